```python
import math
import jax, jax.numpy as jnp
from jax import lax
import numpy as np

D_MODEL = 2048
BATCH = 2
SEQ = 4096
DEPTH = 4
DEC_BATCH = 8
DEC_SEQ = 4
PAST_LEN = 16384
PAGE_SIZE = 128

N_A_LAYERS = DEPTH // 2
N_B_LAYERS = DEPTH - N_A_LAYERS
RET_HEADS = 8
RET_DK = D_MODEL // RET_HEADS
RET_DV = 2 * RET_DK
RET_CHUNK = 128
RET_THETA = 10000.0
NSA_HEADS = 16
NSA_HD = D_MODEL // NSA_HEADS
NSA_KV = 4
NSA_HPG = NSA_HEADS // NSA_KV
CMP_LEN = 32
CMP_STRIDE = 16
SLC_LEN = 64
N_SEL = 16
WINDOW = 512
QBLK = 128
ROPE_THETA = 500000.0
ROT_DIM = NSA_HD // 4
N_KV_PROJ = 6
N_EXPERTS = 16
N_GROUPS = 4
EXPERTS_PER_GROUP = N_EXPERTS // N_GROUPS
TOPK_GROUPS = 1
TOP_K = 2
D_EXPERT = 1408
MOE_BLK = 256
DN_ALPHA = (2.0 * DEPTH) ** 0.25
DN_BETA = (8.0 * DEPTH) ** -0.25
LN_EPS = 1e-5

kernel_name = 'yoco_retention_nsa_grouped_moe_step'

f32 = jnp.float32


def ret_inv_freq():
    return 1.0 / (RET_THETA ** jnp.linspace(0.0, 1.0, RET_DK // 2, dtype=f32))


def nsa_inv_freq():
    return ROPE_THETA ** (-jnp.arange(0, ROT_DIM, 2, dtype=f32) / ROT_DIM)


def rope(x, pos, inv_freq):
    half = inv_freq.shape[0]
    rot = 2 * half
    ang = pos.astype(f32)[:, None] * inv_freq[None, :]
    shape = (ang.shape[0],) + (1,) * (x.ndim - 3) + (half,)
    cos = jnp.cos(ang).reshape(shape)
    sin = jnp.sin(ang).reshape(shape)
    xf = x[..., :rot].astype(f32)
    x1, x2 = xf[..., :half], xf[..., half:]
    out = jnp.concatenate([x1 * cos - x2 * sin, x2 * cos + x1 * sin], axis=-1).astype(x.dtype)
    return jnp.concatenate([out, x[..., rot:]], axis=-1)


def post_norm(x, y, g, b):
    h = DN_ALPHA * x.astype(f32) + y.astype(f32)
    mu = jnp.mean(h, -1, keepdims=True)
    var = jnp.mean(jnp.square(h - mu), -1, keepdims=True)
    return ((h - mu) * lax.rsqrt(var + LN_EPS) * g.astype(f32) + b.astype(f32)).astype(x.dtype)


def masked_softmax(s, mask):
    s = jnp.where(mask, s, -jnp.inf)
    m = jnp.max(s, axis=-1, keepdims=True)
    m = jnp.where(jnp.isfinite(m), m, 0.0)
    e = jnp.exp(s - m)
    den = jnp.sum(e, axis=-1, keepdims=True)
    return e / jnp.where(den > 0, den, 1.0)


def retention_scan(q, k, v, s0, chunk):
    B, S, H, _ = q.shape
    DV = v.shape[-1]
    nc = S // chunk
    log_g = jnp.log(1.0 - 2.0 ** (-5.0 - jnp.arange(H, dtype=f32)))
    i = jnp.arange(chunk, dtype=f32)
    diff = i[:, None] - i[None, :]
    intra = jnp.where(diff >= 0, jnp.exp(log_g[:, None, None] * jnp.maximum(diff, 0.0)), 0.0)
    q_dec = jnp.exp(log_g[:, None] * (i + 1.0))[:, :, None]
    k_dec = jnp.exp(log_g[:, None] * (chunk - 1.0 - i))[:, :, None]
    c_dec = jnp.exp(log_g * chunk)[:, None, None]

    def to_chunks(t):
        return t.reshape(B, nc, chunk, H, t.shape[-1]).transpose(1, 0, 3, 2, 4)

    def step(s, qkv):
        qc, kc, vc = qkv
        att = jnp.einsum('bhid,bhjd->bhij', qc, kc) * intra
        o = jnp.einsum('bhij,bhjv->bhiv', att, vc) + jnp.einsum('bhid,bhdv->bhiv', qc * q_dec, s)
        s = s * c_dec + jnp.einsum('bhjd,bhjv->bhdv', kc * k_dec, vc)
        return s, o

    s_fin, o = lax.scan(step, s0, (to_chunks(q), to_chunks(k), to_chunks(v)))
    return o.transpose(1, 0, 3, 2, 4).reshape(B, S, H, DV), s_fin


def retention_mixer(x, pos, s0, chunk, w_in, w_out):
    B, S, _ = x.shape
    hk, hv = RET_HEADS * RET_DK, RET_HEADS * RET_DV
    proj = jnp.einsum('bsd,de->bse', x, w_in)
    q = proj[..., :hk].reshape(B, S, RET_HEADS, RET_DK)
    k = proj[..., hk:2 * hk].reshape(B, S, RET_HEADS, RET_DK)
    v = proj[..., 2 * hk:2 * hk + hv].reshape(B, S, RET_HEADS, RET_DV)
    g = proj[..., 2 * hk + hv:]
    inv = ret_inv_freq()
    q = rope(q, pos, inv).astype(f32)
    k = rope(k, pos, inv).astype(f32) * (RET_DK ** -0.5)
    o, s_fin = retention_scan(q, k, v.astype(f32), s0.astype(f32), chunk)
    mu = jnp.mean(o, -1, keepdims=True)
    var = jnp.mean(jnp.square(o - mu), -1, keepdims=True)
    o = ((o - mu) * lax.rsqrt(var + LN_EPS)).reshape(B, S, hv)
    h = (jax.nn.silu(g.astype(f32)) * o).astype(x.dtype)
    return h @ w_out, s_fin


def shared_kv_rows(x, pos, w_kv):
    B, S, _ = x.shape
    kv = (x @ w_kv).reshape(B, S, N_KV_PROJ, NSA_KV, NSA_HD)
    inv = nsa_inv_freq()
    k_slc = rope(kv[:, :, 2], pos, inv)
    k_win = rope(kv[:, :, 4], pos, inv)
    paged_rows = jnp.stack([kv[:, :, 0], kv[:, :, 1], k_slc, kv[:, :, 3]], axis=2)
    win_rows = jnp.stack([k_win, kv[:, :, 5]], axis=2)
    return paged_rows, win_rows


def compress(rows, w1, w2, pe):
    B, N, G, d = rows.shape
    r = CMP_LEN // CMP_STRIDE
    nchunk = N // CMP_STRIDE
    ch = rows[:, :nchunk * CMP_STRIDE].reshape(B, nchunk, CMP_STRIDE, G, d)
    blocks = jnp.concatenate([ch[:, i:nchunk - r + 1 + i] for i in range(r)], axis=2)
    nc = blocks.shape[1]
    blocks = blocks + pe[None, None, :, None, :]
    flat = blocks.transpose(0, 1, 3, 2, 4).reshape(B, nc, G, CMP_LEN * d)
    return jax.nn.silu(flat @ w1) @ w2


def cmp_attention(q, kc, vc, qpos):
    nc = kc.shape[1]
    cend = jnp.arange(nc) * CMP_STRIDE + (CMP_LEN - 1)
    mask = cend[None, :] <= qpos[:, None]
    s = jnp.einsum('bqghd,bcgd->bghqc', q, kc, preferred_element_type=f32) * (NSA_HD ** -0.5)
    p = masked_softmax(s, mask)
    o = jnp.einsum('bghqc,bcgd->bqghd', p.astype(vc.dtype), vc)
    return o, p


def select_blocks(p_cmp, qpos, n_total):
    nc = p_cmp.shape[-1]
    ns = -(-n_total // SLC_LEN)
    c0 = jnp.arange(nc) * CMP_STRIDE
    j0 = jnp.arange(ns) * SLC_LEN
    overlap = ((c0[:, None] < j0[None, :] + SLC_LEN) & (c0[:, None] + CMP_LEN > j0[None, :])).astype(f32)
    score = jnp.einsum('bghqc,cj->bgqj', p_cmp, overlap)
    tblk = qpos // SLC_LEN
    j = jnp.arange(ns)
    forced = (j[None, :] == 0) | (j[None, :] == tblk[:, None]) | (j[None, :] == tblk[:, None] - 1)
    score = jnp.where(forced, jnp.inf, jnp.where(j[None, :] > tblk[:, None], -jnp.inf, score))
    _, idx = lax.top_k(score, min(N_SEL, ns))
    return idx


def sel_attention(q, k_sel, v_sel, sel, qpos):
    B, G, Q, n, L, d = k_sel.shape
    H = q.shape[3]
    kpos = sel[..., None] * SLC_LEN + jnp.arange(SLC_LEN)
    mask = (kpos <= qpos[None, None, :, None, None]).reshape(B, G, 1, Q, n * L)
    s = jnp.einsum('bqghd,bgqnkd->bghqnk', q, k_sel, preferred_element_type=f32).reshape(B, G, H, Q, n * L)
    p = masked_softmax(s * (NSA_HD ** -0.5), mask).reshape(B, G, H, Q, n, L)
    return jnp.einsum('bghqnk,bgqnkd->bqghd', p.astype(v_sel.dtype), v_sel)


def attend(q, k, v, mask):
    s = jnp.einsum('...qghd,...kgd->...ghqk', q, k, preferred_element_type=f32) * (NSA_HD ** -0.5)
    p = masked_softmax(s, mask[..., None, None, :, :])
    return jnp.einsum('...ghqk,...kgd->...qghd', p.astype(v.dtype), v)


def window_attention_prompt(q, k, v, qpos):
    B, S, G, H, d = q.shape
    nb, wb = S // QBLK, WINDOW // QBLK
    pad = ((0, 0), (WINDOW, 0), (0, 0), (0, 0))
    kb = jnp.pad(k, pad).reshape(B, nb + wb, QBLK, G, d)
    vb = jnp.pad(v, pad).reshape(B, nb + wb, QBLK, G, d)
    kw = jnp.concatenate([kb[:, i:i + nb] for i in range(wb + 1)], axis=2)
    vw = jnp.concatenate([vb[:, i:i + nb] for i in range(wb + 1)], axis=2)
    qb = q.reshape(B, nb, QBLK, G, H, d)
    qp = qpos.reshape(nb, QBLK)
    kp = jnp.arange(nb)[:, None] * QBLK - WINDOW + jnp.arange((wb + 1) * QBLK)[None, :]
    mask = (kp[:, None, :] >= 0) & (kp[:, None, :] <= qp[:, :, None]) & (kp[:, None, :] >= qp[:, :, None] - WINDOW)
    return attend(qb, kw, vw, mask).reshape(B, S, G, H, d)


def nsa_query(x, pos, w_in):
    B, S, _ = x.shape
    hq = NSA_HEADS * NSA_HD
    proj = x @ w_in
    q = proj[..., :hq].reshape(B, S, NSA_KV, NSA_HPG, NSA_HD)
    gates = jax.nn.sigmoid(proj[..., hq:].astype(f32)).reshape(B, S, NSA_KV, NSA_HPG, 3)
    return q, rope(q, pos, nsa_inv_freq()), gates


def nsa_core(q, q_rot, qpos, kc, vc, n_total, gather):
    o_cmp, p_cmp = cmp_attention(q, kc, vc, qpos)
    sel = select_blocks(p_cmp, qpos, n_total)
    k_sel, v_sel = gather(sel)
    return o_cmp, sel_attention(q_rot, k_sel, v_sel, sel, qpos)


def nsa_merge(gates, o_cmp, o_slc, o_win, w_out, dtype):
    o = gates[..., 0:1] * o_cmp.astype(f32) + gates[..., 1:2] * o_slc.astype(f32) + gates[..., 2:3] * o_win.astype(f32)
    B, S = o.shape[:2]
    return o.reshape(B, S, NSA_HEADS * NSA_HD).astype(dtype) @ w_out


def nsa_prompt(x, pos, rows, win_rows, kc, vc, w_in, w_out):
    B, S, _ = x.shape
    q, qr, gates = nsa_query(x, pos, w_in)
    k_slc, v_slc = rows[:, :, 2], rows[:, :, 3]
    bi = jnp.arange(B)[:, None, None, None, None]
    gi = jnp.arange(NSA_KV)[None, :, None, None, None]
    ar = jnp.arange(SLC_LEN)

    def gather(sel):
        r = sel[..., None] * SLC_LEN + ar
        return k_slc[bi, r, gi], v_slc[bi, r, gi]

    nb = S // QBLK

    def split(t):
        return t.reshape((B, nb, QBLK) + t.shape[2:]).swapaxes(0, 1)

    def block(args):
        qb, qrb, pb = args
        return nsa_core(qb, qrb, pb, kc, vc, S, gather)

    o_cmp, o_slc = lax.map(block, (split(q), split(qr), pos.reshape(nb, QBLK)))
    merge = lambda t: t.swapaxes(0, 1).reshape(B, S, NSA_KV, NSA_HPG, NSA_HD)
    o_win = window_attention_prompt(qr, win_rows[:, :, 0], win_rows[:, :, 1], pos)
    return nsa_merge(gates, merge(o_cmp), merge(o_slc), o_win, w_out, x.dtype)


def nsa_sample(x, pos, rows, win_keys, kc, vc, cache_kv, page_table, w_in, w_out):
    DB, T, _ = x.shape
    past = page_table.shape[1] * PAGE_SIZE
    q, qr, gates = nsa_query(x, pos, w_in)
    sub = PAGE_SIZE // SLC_LEN
    npb = past // SLC_LEN
    ns = -(-(past + T) // SLC_LEN)
    nt = ns - npb
    tail = jnp.pad(rows, ((0, 0), (0, nt * SLC_LEN - T), (0, 0), (0, 0), (0, 0)))
    bi4 = jnp.arange(DB)[:, None, None, None]
    bi = bi4[..., None]
    gi = jnp.arange(NSA_KV)[None, :, None, None, None]
    ar = jnp.arange(SLC_LEN)

    def gather(sel):
        jp = jnp.clip(sel, 0, npb - 1)
        phys = page_table[bi4, jp // sub][..., None]
        rp = (jp % sub)[..., None] * SLC_LEN + ar
        rt = jnp.clip(sel - npb, 0, nt - 1)[..., None] * SLC_LEN + ar
        in_past = (sel < npb)[..., None, None]
        k = jnp.where(in_past, cache_kv[phys, rp, 2, gi], tail[bi, rt, 2, gi])
        v = jnp.where(in_past, cache_kv[phys, rp, 3, gi], tail[bi, rt, 3, gi])
        return k, v

    o_cmp, o_slc = nsa_core(q, qr, pos, kc, vc, past + T, gather)
    wb = win_keys.shape[1] - T
    kpos = past - wb + jnp.arange(wb + T)
    mask = (kpos[None, :] <= pos[:, None]) & (kpos[None, :] >= pos[:, None] - WINDOW)
    o_win = attend(qr, win_keys[:, :, 0], win_keys[:, :, 1], mask)
    return nsa_merge(gates, o_cmp, o_slc, o_win, w_out, x.dtype)


def moe(x, w_router, b_router, w_g, w_u, w_d):
    T, D = x.shape
    aff = jax.nn.sigmoid(jnp.einsum('td,de->te', x, w_router, preferred_element_type=f32))
    biased = aff + b_router.astype(f32)
    grp = biased.reshape(T, N_GROUPS, EXPERTS_PER_GROUP)
    grp_score = lax.top_k(grp, 2)[0].sum(-1)
    _, top_g = lax.top_k(grp_score, TOPK_GROUPS)
    g_mask = jnp.any(top_g[:, :, None] == jnp.arange(N_GROUPS)[None, None, :], axis=1)
    e_mask = jnp.repeat(g_mask, EXPERTS_PER_GROUP, axis=1)
    _, idx = lax.top_k(jnp.where(e_mask, biased, -jnp.inf), TOP_K)
    w = jnp.take_along_axis(aff, idx, axis=1)
    w = w / jnp.sum(w, -1, keepdims=True)
    A = T * TOP_K
    e_flat = idx.reshape(A)
    t_flat = jnp.repeat(jnp.arange(T), TOP_K)
    order = jnp.argsort(e_flat)
    e_s, t_s, w_s = e_flat[order], t_flat[order], w.reshape(A)[order]
    counts = jnp.bincount(e_flat, length=N_EXPERTS)
    start = jnp.cumsum(counts) - counts
    padded = (counts + MOE_BLK - 1) // MOE_BLK * MOE_BLK
    pend = jnp.cumsum(padded)
    pstart = pend - padded
    dest = pstart[e_s] + (jnp.arange(A) - start[e_s])
    n_blk = -(-A // MOE_BLK) + N_EXPERTS
    xp = jnp.zeros((n_blk * MOE_BLK, D), x.dtype).at[dest].set(x[t_s])
    blk_e = jnp.minimum(jnp.searchsorted(pend, jnp.arange(n_blk) * MOE_BLK, side='right'), N_EXPERTS - 1)

    def expert_block(args):
        xb, e = args
        h = jax.nn.silu(xb @ w_g[e]) * (xb @ w_u[e])
        return h @ w_d[e]

    out = lax.map(expert_block, (xp.reshape(n_blk, MOE_BLK, D), blk_e)).reshape(n_blk * MOE_BLK, D)
    y = out[dest] * w_s[:, None].astype(x.dtype)
    return jax.ops.segment_sum(y, t_s, num_segments=T)


def setup_inputs(seed: int = 0) -> dict:
    key = jax.random.key(seed)
    ks = jax.random.split(key, 24)
    n_pages = PAST_LEN // PAGE_SIZE
    n_pool = (DEC_BATCH * n_pages * 5) // 4
    wb = min(WINDOW, PAST_LEN)
    hk, hv = RET_HEADS * RET_DK, RET_HEADS * RET_DV
    hq = NSA_HEADS * NSA_HD

    def nrm(k, shape, scale):
        return jax.random.normal(k, shape, f32) * scale

    col_scale = jnp.concatenate([jnp.ones((2 * hk,), f32), jnp.full((hv,), DN_BETA, f32), jnp.ones((hv,), f32)])
    kv_scale = jnp.array([1.0, DN_BETA] * 3, f32)
    page_table = jax.random.permutation(ks[5], n_pool)[:DEC_BATCH * n_pages].reshape(DEC_BATCH, n_pages).astype(jnp.int32)
    return {
        'x_prompt': nrm(ks[0], (BATCH, SEQ, D_MODEL), 1.0),
        'x_sample': nrm(ks[1], (DEC_BATCH, DEC_SEQ, D_MODEL), 1.0),
        'cache_kv': nrm(ks[2], (n_pool, PAGE_SIZE, 4, NSA_KV, NSA_HD), 1.0),
        'cache_win': nrm(ks[3], (DEC_BATCH, wb, 2, NSA_KV, NSA_HD), 1.0),
        'state_ret': nrm(ks[4], (N_A_LAYERS, DEC_BATCH, RET_HEADS, RET_DK, RET_DV), 0.1),
        'page_table': page_table,
        'w_ret_in': nrm(ks[6], (N_A_LAYERS, D_MODEL, 2 * hk + 2 * hv), D_MODEL ** -0.5) * col_scale,
        'w_ret_out': nrm(ks[7], (N_A_LAYERS, hv, D_MODEL), hv ** -0.5 * DN_BETA),
        'w_kv': (nrm(ks[8], (D_MODEL, N_KV_PROJ, NSA_KV * NSA_HD), D_MODEL ** -0.5) * kv_scale[None, :, None]).reshape(D_MODEL, N_KV_PROJ * NSA_KV * NSA_HD),
        'w_cmp_k1': nrm(ks[9], (CMP_LEN * NSA_HD, NSA_HD), (CMP_LEN * NSA_HD) ** -0.5),
        'w_cmp_k2': nrm(ks[10], (NSA_HD, NSA_HD), NSA_HD ** -0.5),
        'pe_cmp_k': nrm(ks[11], (CMP_LEN, NSA_HD), 0.1),
        'w_cmp_v1': nrm(ks[12], (CMP_LEN * NSA_HD, NSA_HD), (CMP_LEN * NSA_HD) ** -0.5),
        'w_cmp_v2': nrm(ks[13], (NSA_HD, NSA_HD), NSA_HD ** -0.5),
        'pe_cmp_v': nrm(ks[14], (CMP_LEN, NSA_HD), 0.1),
        'w_nsa_in': nrm(ks[15], (N_B_LAYERS, D_MODEL, hq + 3 * NSA_HEADS), D_MODEL ** -0.5),
        'w_nsa_out': nrm(ks[16], (N_B_LAYERS, hq, D_MODEL), hq ** -0.5 * DN_BETA),
        'w_router': nrm(ks[17], (D_MODEL, N_EXPERTS), D_MODEL ** -0.5),
        'b_router': nrm(ks[18], (N_EXPERTS,), 0.01),
        'w_exp_gate': nrm(ks[19], (DEPTH, N_EXPERTS, D_MODEL, D_EXPERT), D_MODEL ** -0.5),
        'w_exp_up': nrm(ks[20], (DEPTH, N_EXPERTS, D_MODEL, D_EXPERT), D_MODEL ** -0.5),
        'w_exp_down': nrm(ks[21], (DEPTH, N_EXPERTS, D_EXPERT, D_MODEL), D_EXPERT ** -0.5 * DN_BETA),
        'ln_gain': 1.0 + nrm(ks[22], (DEPTH, 2, D_MODEL), 0.02),
        'ln_bias': nrm(ks[23], (DEPTH, 2, D_MODEL), 0.02),
    }


def reference(x_prompt, x_sample, cache_kv, cache_win, state_ret, page_table,
              w_ret_in, w_ret_out, w_kv, w_cmp_k1, w_cmp_k2, pe_cmp_k, w_cmp_v1, w_cmp_v2, pe_cmp_v,
              w_nsa_in, w_nsa_out, w_router, b_router, w_exp_gate, w_exp_up, w_exp_down, ln_gain, ln_bias):
    B, S, D = x_prompt.shape
    DB, T, _ = x_sample.shape
    past = page_table.shape[1] * PAGE_SIZE
    wb = cache_win.shape[1]
    pos_p = jnp.arange(S, dtype=jnp.int32)
    pos_s = past + jnp.arange(T, dtype=jnp.int32)
    xp, xs = x_prompt, x_sample
    ret_p, ret_s = [], []
    for l in range(DEPTH):
        if l < N_A_LAYERS:
            s0 = jnp.zeros((B, RET_HEADS, RET_DK, RET_DV), f32)
            yp, sp = retention_mixer(xp, pos_p, s0, min(RET_CHUNK, S), w_ret_in[l], w_ret_out[l])
            ys, ss = retention_mixer(xs, pos_s, state_ret[l], T, w_ret_in[l], w_ret_out[l])
            ret_p.append(sp.astype(state_ret.dtype))
            ret_s.append(ss.astype(state_ret.dtype))
        else:
            if l == N_A_LAYERS:
                rows_p, win_rows_p = shared_kv_rows(xp, pos_p, w_kv)
                rows_s, win_rows_s = shared_kv_rows(xs, pos_s, w_kv)
                kc_p = compress(rows_p[:, :, 0], w_cmp_k1, w_cmp_k2, pe_cmp_k)
                vc_p = compress(rows_p[:, :, 1], w_cmp_v1, w_cmp_v2, pe_cmp_v)
                past_cmp = cache_kv[page_table, :, :2].reshape(DB, past, 2, NSA_KV, NSA_HD)
                kc_s = compress(jnp.concatenate([past_cmp[:, :, 0], rows_s[:, :, 0]], axis=1), w_cmp_k1, w_cmp_k2, pe_cmp_k)
                vc_s = compress(jnp.concatenate([past_cmp[:, :, 1], rows_s[:, :, 1]], axis=1), w_cmp_v1, w_cmp_v2, pe_cmp_v)
                win_keys_s = jnp.concatenate([cache_win, win_rows_s], axis=1)
            li = l - N_A_LAYERS
            yp = nsa_prompt(xp, pos_p, rows_p, win_rows_p, kc_p, vc_p, w_nsa_in[li], w_nsa_out[li])
            ys = nsa_sample(xs, pos_s, rows_s, win_keys_s, kc_s, vc_s, cache_kv, page_table, w_nsa_in[li], w_nsa_out[li])
        xp = post_norm(xp, yp, ln_gain[l, 0], ln_bias[l, 0])
        xs = post_norm(xs, ys, ln_gain[l, 0], ln_bias[l, 0])
        xt = jnp.concatenate([xp.reshape(B * S, D), xs.reshape(DB * T, D)], axis=0)
        yt = moe(xt, w_router, b_router, w_exp_gate[l], w_exp_up[l], w_exp_down[l])
        xp = post_norm(xp, yt[:B * S].reshape(B, S, D), ln_gain[l, 1], ln_bias[l, 1])
        xs = post_norm(xs, yt[B * S:].reshape(DB, T, D), ln_gain[l, 1], ln_bias[l, 1])
    if S >= wb:
        win_p = win_rows_p[:, S - wb:]
    else:
        win_p = jnp.pad(win_rows_p, ((0, 0), (wb - S, 0), (0, 0), (0, 0), (0, 0)))
    win_s = win_keys_s[:, T:]
    return (xp, xs, rows_p, rows_s, win_p, win_s, jnp.stack(ret_p), jnp.stack(ret_s))
```

```python
import functools

import jax
import jax.numpy as jnp
from jax import lax
from jax.experimental import pallas as pl
from jax.experimental.pallas import tpu as pltpu

f32 = jnp.float32
bf16 = jnp.bfloat16

D_MODEL = 2048
DEPTH = 4
PAGE_SIZE = 128
N_A_LAYERS = DEPTH // 2
RET_HEADS = 8
RET_DK = D_MODEL // RET_HEADS
RET_DV = 2 * RET_DK
RET_CHUNK = 128
RET_THETA = 10000.0
NSA_HEADS = 16
NSA_HD = D_MODEL // NSA_HEADS
NSA_KV = 4
NSA_HPG = NSA_HEADS // NSA_KV
CMP_LEN = 32
CMP_STRIDE = 16
SLC_LEN = 64
N_SEL = 16
WINDOW = 512
QBLK = 128
ROPE_THETA = 500000.0
ROT_DIM = NSA_HD // 4
N_KV_PROJ = 6
N_EXPERTS = 16
N_GROUPS = 4
EXPERTS_PER_GROUP = N_EXPERTS // N_GROUPS
TOPK_GROUPS = 1
TOP_K = 2
D_EXPERT = 1408
DN_ALPHA = (2.0 * DEPTH) ** 0.25
LN_EPS = 1e-5

TP = 16
MOE_TM = 512
MOE_TF = 128
VMEM_LIMIT = 56 * 1024 * 1024
NEG = -1e30


def _cparams(sem):
    return pltpu.CompilerParams(dimension_semantics=sem, vmem_limit_bytes=VMEM_LIMIT)


def _bdot(a, b):
    return jnp.dot(a.astype(bf16), b.astype(bf16), preferred_element_type=f32)


def _bdot_nt(a, b):
    return lax.dot_general(a.astype(bf16), b.astype(bf16), (((1,), (1,)), ((), ())), preferred_element_type=f32)


def _mm_kernel(x_ref, w_ref, o_ref):
    o_ref[...] = _bdot(x_ref[...], w_ref[...])


def _mm3_kernel(x_ref, w_ref, o_ref):
    x = x_ref[...]
    w = w_ref[...]
    xh = x.astype(bf16)
    wh = w.astype(bf16)
    xl = (x - xh.astype(f32)).astype(bf16)
    wl = (w - wh.astype(f32)).astype(bf16)
    acc = jnp.dot(xh, wh, preferred_element_type=f32)
    acc += jnp.dot(xh, wl, preferred_element_type=f32)
    acc += jnp.dot(xl, wh, preferred_element_type=f32)
    o_ref[...] = acc


def _row_tile(m, pref=640):
    for t in (pref, 512, 256, 128):
        if m % t == 0:
            return t
    return min(512, m)


def _mm(x, w, layer=None, tn=1024, body=_mm_kernel):
    M, K = x.shape
    N = w.shape[-1]
    tn = min(tn, N)
    tm = _row_tile(M)
    if layer is None:
        w_spec = pl.BlockSpec((K, tn), lambda j, i: (0, j))
    else:
        w_spec = pl.BlockSpec((None, K, tn), lambda j, i: (layer, 0, j))
    return pl.pallas_call(
        body,
        grid=(pl.cdiv(N, tn), pl.cdiv(M, tm)),
        in_specs=[pl.BlockSpec((tm, K), lambda j, i: (i, 0)), w_spec],
        out_specs=pl.BlockSpec((tm, tn), lambda j, i: (i, j)),
        out_shape=jax.ShapeDtypeStruct((M, N), f32),
        compiler_params=_cparams(("parallel", "parallel")),
    )(x, w)


def _pn_kernel(x_ref, y_ref, g_ref, b_ref, o_ref):
    h = DN_ALPHA * x_ref[...] + y_ref[...]
    mu = jnp.mean(h, -1, keepdims=True)
    d = h - mu
    var = jnp.mean(d * d, -1, keepdims=True)
    o_ref[...] = d * lax.rsqrt(var + LN_EPS) * g_ref[...] + b_ref[...]


def _post_norm(x, y, g, b):
    M, D = x.shape
    tm = _row_tile(M, 128)
    row = pl.BlockSpec((tm, D), lambda i: (i, 0))
    vec = pl.BlockSpec((1, D), lambda i: (0, 0))
    return pl.pallas_call(
        _pn_kernel,
        grid=(pl.cdiv(M, tm),),
        in_specs=[row, row, vec, vec],
        out_specs=row,
        out_shape=jax.ShapeDtypeStruct((M, D), f32),
        compiler_params=_cparams(("parallel",)),
    )(x, y, g.reshape(1, D), b.reshape(1, D))


def _ret_kernel(q_ref, k_ref, v_ref, g_ref, cos_ref, sin_ref, intra_ref, qd_ref, kd_ref, cd_ref, s0_ref,
                o_ref, sfin_ref, s_scr, *, nc):
    c = pl.program_id(2)

    @pl.when(c == 0)
    def _():
        s_scr[...] = s0_ref[...]

    half = RET_DK // 2
    cos = cos_ref[...]
    sin = sin_ref[...]

    def rot(t):
        t1, t2 = t[:, :half], t[:, half:]
        return jnp.concatenate([t1 * cos - t2 * sin, t2 * cos + t1 * sin], axis=-1)

    q = rot(q_ref[...])
    k = rot(k_ref[...]) * (RET_DK ** -0.5)
    v = v_ref[...].astype(bf16)
    s = s_scr[...]
    att = _bdot_nt(q, k) * intra_ref[...]
    o = _bdot(att, v) + _bdot(q * qd_ref[...], s)
    kd = (k * kd_ref[...]).T
    s_new = s * cd_ref[...] + _bdot(kd, v)
    s_scr[...] = s_new
    mu = jnp.mean(o, -1, keepdims=True)
    d = o - mu
    var = jnp.mean(d * d, -1, keepdims=True)
    gate = g_ref[...]
    o_ref[...] = gate * jax.nn.sigmoid(gate) * (d * lax.rsqrt(var + LN_EPS))

    @pl.when(c == nc - 1)
    def _():
        sfin_ref[...] = s_new


def _ret_decay(chunk, valid):
    log_g = jnp.log(1.0 - 2.0 ** (-5.0 - jnp.arange(RET_HEADS, dtype=f32)))
    i = jnp.arange(chunk, dtype=f32)
    real = i < valid
    diff = i[:, None] - i[None, :]
    intra = jnp.where((diff >= 0) & real[None, :], jnp.exp(log_g[:, None, None] * jnp.maximum(diff, 0.0)), 0.0)
    q_dec = jnp.exp(log_g[:, None] * (i + 1.0))[:, :, None]
    k_dec = jnp.where(real, jnp.exp(log_g[:, None] * jnp.maximum(valid - 1.0 - i, 0.0)), 0.0)[:, :, None]
    c_dec = jnp.exp(log_g * valid)[:, None, None]
    return intra, q_dec, k_dec, c_dec


def _ret_tables(pos):
    inv = 1.0 / (RET_THETA ** jnp.linspace(0.0, 1.0, RET_DK // 2, dtype=f32))
    ang = pos.astype(f32)[:, None] * inv[None, :]
    return jnp.cos(ang), jnp.sin(ang)


def _retention(proj, pos, s0, nb, nc, chunk, valid, row0, o_prev):
    Ttot = proj.shape[0]
    hk, hv = RET_HEADS * RET_DK, RET_HEADS * RET_DV
    rb0 = row0 // chunk
    cos, sin = _ret_tables(pos)
    intra, q_dec, k_dec, c_dec = _ret_decay(chunk, valid)
    H = RET_HEADS

    def rowblk(b, h, c):
        return rb0 + b * nc + c

    in_specs = [
        pl.BlockSpec((chunk, RET_DK), lambda b, h, c: (rowblk(b, h, c), h)),
        pl.BlockSpec((chunk, RET_DK), lambda b, h, c: (rowblk(b, h, c), H + h)),
        pl.BlockSpec((chunk, RET_DV), lambda b, h, c: (rowblk(b, h, c), (2 * hk) // RET_DV + h)),
        pl.BlockSpec((chunk, RET_DV), lambda b, h, c: (rowblk(b, h, c), (2 * hk + hv) // RET_DV + h)),
        pl.BlockSpec((chunk, RET_DK // 2), lambda b, h, c: (c, 0)),
        pl.BlockSpec((chunk, RET_DK // 2), lambda b, h, c: (c, 0)),
        pl.BlockSpec((None, chunk, chunk), lambda b, h, c: (h, 0, 0)),
        pl.BlockSpec((None, chunk, 1), lambda b, h, c: (h, 0, 0)),
        pl.BlockSpec((None, chunk, 1), lambda b, h, c: (h, 0, 0)),
        pl.BlockSpec((None, 1, 1), lambda b, h, c: (h, 0, 0)),
        pl.BlockSpec((None, None, RET_DK, RET_DV), lambda b, h, c: (b, h, 0, 0)),
    ]
    args = [proj, proj, proj, proj, cos, sin, intra, q_dec, k_dec, c_dec, s0]
    aliases = {}
    if o_prev is not None:
        in_specs.append(pl.BlockSpec(memory_space=pl.ANY))
        args.append(o_prev)
        aliases = {len(args) - 1: 0}

    def body(*refs):
        if o_prev is not None:
            refs = refs[:11] + refs[12:]
        _ret_kernel(*refs, nc=nc)

    return pl.pallas_call(
        body,
        grid=(nb, H, nc),
        in_specs=in_specs,
        out_specs=[
            pl.BlockSpec((chunk, RET_DV), lambda b, h, c: (rowblk(b, h, c), h)),
            pl.BlockSpec((None, None, RET_DK, RET_DV), lambda b, h, c: (b, h, 0, 0)),
        ],
        out_shape=[jax.ShapeDtypeStruct((Ttot, hv), f32), jax.ShapeDtypeStruct((nb, H, RET_DK, RET_DV), f32)],
        scratch_shapes=[pltpu.VMEM((RET_DK, RET_DV), f32)],
        input_output_aliases=aliases,
        compiler_params=_cparams(("parallel", "parallel", "arbitrary")),
    )(*args)


def _moe_kernel(be_ref, nu_ref, x_ref, wg_ref, wu_ref, wd_ref, o_ref, xb_scr):
    i = pl.program_id(0)
    j = pl.program_id(1)

    @pl.when(i < nu_ref[0])
    def _():
        @pl.when(j == 0)
        def _():
            xb_scr[...] = x_ref[...].astype(bf16)

        xb = xb_scr[...]
        g = jnp.dot(xb, wg_ref[...].astype(bf16), preferred_element_type=f32)
        u = jnp.dot(xb, wu_ref[...].astype(bf16), preferred_element_type=f32)
        h = g * jax.nn.sigmoid(g) * u
        y = _bdot(h, wd_ref[...])

        @pl.when(j == 0)
        def _():
            o_ref[...] = y

        @pl.when(j > 0)
        def _():
            o_ref[...] += y


def _moe_experts(xs, blk_e, n_used, w_g, w_u, w_d, layer):
    R, D = xs.shape
    n_blk = R // MOE_TM
    nf = D_EXPERT // MOE_TF

    def ii(i, nu):
        return jnp.minimum(i, nu[0] - 1)

    def jj(i, j, nu):
        return jnp.where(i < nu[0], j, nf - 1)

    grid_spec = pltpu.PrefetchScalarGridSpec(
        num_scalar_prefetch=2,
        grid=(n_blk, nf),
        in_specs=[
            pl.BlockSpec((MOE_TM, D), lambda i, j, be, nu: (ii(i, nu), 0)),
            pl.BlockSpec((None, None, D, MOE_TF), lambda i, j, be, nu: (layer, be[ii(i, nu)], 0, jj(i, j, nu))),
            pl.BlockSpec((None, None, D, MOE_TF), lambda i, j, be, nu: (layer, be[ii(i, nu)], 0, jj(i, j, nu))),
            pl.BlockSpec((None, None, MOE_TF, D), lambda i, j, be, nu: (layer, be[ii(i, nu)], jj(i, j, nu), 0)),
        ],
        out_specs=pl.BlockSpec((MOE_TM, D), lambda i, j, be, nu: (ii(i, nu), 0)),
        scratch_shapes=[pltpu.VMEM((MOE_TM, D), bf16)],
    )
    return pl.pallas_call(
        _moe_kernel,
        grid_spec=grid_spec,
        out_shape=jax.ShapeDtypeStruct((R, D), f32),
        compiler_params=_cparams(("arbitrary", "arbitrary")),
    )(blk_e, n_used, xs, w_g, w_u, w_d)


def _moe(x, w_router, b_router, w_g, w_u, w_d, layer):
    T, D = x.shape
    logits = _mm(x, w_router, body=_mm3_kernel)
    aff = jax.nn.sigmoid(logits)
    biased = aff + b_router.astype(f32)
    grp = biased.reshape(T, N_GROUPS, EXPERTS_PER_GROUP)
    grp_score = lax.top_k(grp, 2)[0].sum(-1)
    _, top_g = lax.top_k(grp_score, TOPK_GROUPS)
    g_mask = jnp.any(top_g[:, :, None] == jnp.arange(N_GROUPS)[None, None, :], axis=1)
    e_mask = jnp.repeat(g_mask, EXPERTS_PER_GROUP, axis=1)
    _, idx = lax.top_k(jnp.where(e_mask, biased, -jnp.inf), TOP_K)
    w = jnp.take_along_axis(aff, idx, axis=1)
    w = w / jnp.sum(w, -1, keepdims=True)
    A = T * TOP_K
    e_flat = idx.reshape(A).astype(jnp.int32)
    order = jnp.argsort(e_flat)
    e_s = e_flat[order]
    counts = jnp.bincount(e_flat, length=N_EXPERTS)
    start = jnp.cumsum(counts) - counts
    padded = (counts + MOE_TM - 1) // MOE_TM * MOE_TM
    pend = jnp.cumsum(padded)
    pstart = pend - padded
    dest = (pstart[e_s] + (jnp.arange(A) - start[e_s])).astype(jnp.int32)
    n_blk = -(-A // MOE_TM) + N_EXPERTS
    rows_src = jnp.full((n_blk * MOE_TM,), T, jnp.int32).at[dest].set((order // TOP_K).astype(jnp.int32))
    xs = jnp.concatenate([x, jnp.zeros((1, D), x.dtype)], axis=0)[rows_src]
    blk_e = jnp.minimum(jnp.searchsorted(pend, jnp.arange(n_blk) * MOE_TM, side='right'), N_EXPERTS - 1)
    n_used = (pend[-1] // MOE_TM).astype(jnp.int32).reshape(1)
    out = _moe_experts(xs, blk_e.astype(jnp.int32), n_used, w_g, w_u, w_d, layer)
    pos = jnp.zeros((A,), jnp.int32).at[order].set(dest).reshape(T, TOP_K)
    return out[pos[:, 0]] * w[:, 0:1] + out[pos[:, 1]] * w[:, 1:2]


def _nsa_inv_freq():
    return ROPE_THETA ** (-jnp.arange(0, ROT_DIM, 2, dtype=f32) / ROT_DIM)


def _rope_heads(x, pos):
    T = x.shape[0]
    half = ROT_DIM // 2
    ang = pos.astype(f32)[:, None] * _nsa_inv_freq()[None, :]
    cos = jnp.cos(ang)[:, None, :]
    sin = jnp.sin(ang)[:, None, :]
    xh = x.reshape(T, -1, NSA_HD)
    x1, x2 = xh[..., :half], xh[..., half:ROT_DIM]
    out = jnp.concatenate([x1 * cos - x2 * sin, x2 * cos + x1 * sin, xh[..., ROT_DIM:]], axis=-1)
    return out.reshape(x.shape)


def _masked_softmax(s, mask):
    s = jnp.where(mask, s, NEG)
    m = jnp.max(s, axis=-1, keepdims=True)
    m = jnp.where(m <= 0.5 * NEG, 0.0, m)
    e = jnp.where(mask, jnp.exp(s - m), 0.0)
    den = jnp.sum(e, axis=-1, keepdims=True)
    return e / jnp.where(den > 0, den, 1.0)


def _compress(chunks, w1, w2, pe):
    n, nchunk, kk = chunks.shape
    wab = jnp.concatenate([w1[:kk], w1[kk:]], axis=1)
    y = _mm(chunks.reshape(n * nchunk, kk), wab).reshape(n, nchunk, 2 * NSA_HD)
    bias = _mm(jnp.broadcast_to(pe.reshape(1, CMP_LEN * NSA_HD), (8, CMP_LEN * NSA_HD)), w1)[:1]
    pre = y[:, :-1, :NSA_HD] + y[:, 1:, NSA_HD:] + bias
    nc = nchunk - 1
    return _mm(jax.nn.silu(pre).reshape(n * nc, NSA_HD), w2).reshape(n, nc, NSA_HD)


def _cmp_kernel(q_ref, kc_ref, vc_ref, ov_ref, o_ref, sc_ref):
    qi = pl.program_id(2)
    ncp = kc_ref.shape[0]
    qpos = qi * QBLK + lax.broadcasted_iota(jnp.int32, (QBLK, ncp), 0)
    cend = lax.broadcasted_iota(jnp.int32, (QBLK, ncp), 1) * CMP_STRIDE + (CMP_LEN - 1)
    mask = cend <= qpos
    kc = kc_ref[...].astype(bf16)
    vc = vc_ref[...].astype(bf16)
    ov = ov_ref[...]
    score = jnp.zeros((QBLK, ov.shape[1]), f32)
    for h in range(NSA_HPG):
        q = q_ref[:, h * NSA_HD:(h + 1) * NSA_HD]
        s = _bdot_nt(q, kc) * (NSA_HD ** -0.5)
        p = _masked_softmax(s, mask).astype(bf16)
        o_ref[:, h * NSA_HD:(h + 1) * NSA_HD] = jnp.dot(p, vc, preferred_element_type=f32)
        score += jnp.dot(p, ov, preferred_element_type=f32)
    sc_ref[...] = score


def _cmp_prompt(q, kc, vc, overlap, B, S):
    Ttot = q.shape[0]
    nq = S // QBLK
    ncp = kc.shape[1]
    ns = overlap.shape[1]
    G = NSA_KV
    gw = NSA_HPG * NSA_HD
    return pl.pallas_call(
        _cmp_kernel,
        grid=(B, G, nq),
        in_specs=[
            pl.BlockSpec((QBLK, gw), lambda b, g, i: (b * nq + i, g)),
            pl.BlockSpec((None, ncp, NSA_HD), lambda b, g, i: (b * G + g, 0, 0)),
            pl.BlockSpec((None, ncp, NSA_HD), lambda b, g, i: (b * G + g, 0, 0)),
            pl.BlockSpec((ncp, ns), lambda b, g, i: (0, 0)),
        ],
        out_specs=[
            pl.BlockSpec((QBLK, gw), lambda b, g, i: (b * nq + i, g)),
            pl.BlockSpec((None, QBLK, ns), lambda b, g, i: (b * G + g, i, 0)),
        ],
        out_shape=[jax.ShapeDtypeStruct((Ttot, G * gw), f32), jax.ShapeDtypeStruct((B * G, S, ns), f32)],
        compiler_params=_cparams(("parallel", "parallel", "parallel")),
    )(q, kc, vc, overlap)


SEL_TK = 512


def _slc_win_kernel(q_ref, sel_ref, ks_ref, vs_ref, kw_ref, vw_ref, os_ref, ow_ref, m_scr, l_scr, acc_scr):
    qi = pl.program_id(2)
    scale = NSA_HD ** -0.5
    qs = [(q_ref[:, h * NSA_HD:(h + 1) * NSA_HD] * scale).astype(bf16) for h in range(NSA_HPG)]
    sel = sel_ref[...].astype(bf16)
    ns = sel.shape[1]
    bpt = SEL_TK // SLC_LEN

    m_scr[...] = jnp.full(m_scr.shape, NEG, f32)
    l_scr[...] = jnp.zeros(l_scr.shape, f32)
    acc_scr[...] = jnp.zeros(acc_scr.shape, f32)
    qpos = qi * QBLK + lax.broadcasted_iota(jnp.int32, (QBLK, SEL_TK), 0)
    n_tiles = (qi * QBLK + QBLK - 1) // SEL_TK + 1

    def tile(t, carry):
        k0 = pl.multiple_of(t * SEL_TK, SEL_TK)
        k = ks_ref[pl.ds(k0, SEL_TK), :].astype(bf16)
        v = vs_ref[pl.ds(k0, SEL_TK), :].astype(bf16)
        blk = lax.broadcasted_iota(jnp.int32, (ns, SEL_TK), 1) // SLC_LEN + t * bpt
        expand = jnp.where(blk == lax.broadcasted_iota(jnp.int32, (ns, SEL_TK), 0), 1.0, 0.0).astype(bf16)
        chosen = jnp.dot(sel, expand, preferred_element_type=f32) > 0.5
        kpos = k0 + lax.broadcasted_iota(jnp.int32, (QBLK, SEL_TK), 1)
        mask = chosen & (kpos <= qpos)
        for h in range(NSA_HPG):
            s = jnp.where(mask, _bdot_nt(qs[h], k), NEG)
            m_old = m_scr[h]
            m_new = jnp.maximum(m_old, jnp.max(s, axis=-1, keepdims=True))
            p = jnp.where(mask, jnp.exp(s - m_new), 0.0)
            a = jnp.exp(m_old - m_new)
            l_scr[h] = a * l_scr[h] + jnp.sum(p, axis=-1, keepdims=True)
            acc_scr[h] = a * acc_scr[h] + jnp.dot(p.astype(bf16), v, preferred_element_type=f32)
            m_scr[h] = m_new
        return carry

    lax.fori_loop(0, n_tiles, tile, 0)
    for h in range(NSA_HPG):
        l = l_scr[h]
        os_ref[:, h * NSA_HD:(h + 1) * NSA_HD] = acc_scr[h] / jnp.where(l > 0, l, 1.0)

    wk = WINDOW + QBLK
    w0 = pl.multiple_of(jnp.maximum(qi * QBLK - WINDOW, 0), QBLK)
    kw = kw_ref[pl.ds(w0, wk), :].astype(bf16)
    vw = vw_ref[pl.ds(w0, wk), :].astype(bf16)
    qp = qi * QBLK + lax.broadcasted_iota(jnp.int32, (QBLK, wk), 0)
    kp = w0 + lax.broadcasted_iota(jnp.int32, (QBLK, wk), 1)
    wmask = (kp <= qp) & (kp >= qp - WINDOW)
    for h in range(NSA_HPG):
        p = _masked_softmax(_bdot_nt(qs[h], kw), wmask)
        ow_ref[:, h * NSA_HD:(h + 1) * NSA_HD] = jnp.dot(p.astype(bf16), vw, preferred_element_type=f32)


def _slc_win_prompt(qr, selmask, rows, win_rows, B, S):
    Ttot = qr.shape[0]
    nq = S // QBLK
    ns = selmask.shape[-1]
    G = NSA_KV
    gw = NSA_HPG * NSA_HD
    qspec = pl.BlockSpec((QBLK, gw), lambda b, g, i: (b * nq + i, g))
    return pl.pallas_call(
        _slc_win_kernel,
        grid=(B, G, nq),
        in_specs=[
            qspec,
            pl.BlockSpec((None, QBLK, ns), lambda b, g, i: (b * G + g, i, 0)),
            pl.BlockSpec((S, NSA_HD), lambda b, g, i: (b, 2 * G + g)),
            pl.BlockSpec((S, NSA_HD), lambda b, g, i: (b, 3 * G + g)),
            pl.BlockSpec((S, NSA_HD), lambda b, g, i: (b, g)),
            pl.BlockSpec((S, NSA_HD), lambda b, g, i: (b, G + g)),
        ],
        out_specs=[qspec, qspec],
        out_shape=[jax.ShapeDtypeStruct((Ttot, G * gw), f32)] * 2,
        scratch_shapes=[
            pltpu.VMEM((NSA_HPG, QBLK, 1), f32),
            pltpu.VMEM((NSA_HPG, QBLK, 1), f32),
            pltpu.VMEM((NSA_HPG, QBLK, NSA_HD), f32),
        ],
        compiler_params=_cparams(("parallel", "parallel", "arbitrary")),
    )(qr, selmask, rows, rows, win_rows, win_rows)


def _overlap(nc, ns):
    c0 = jnp.arange(nc) * CMP_STRIDE
    j0 = jnp.arange(ns) * SLC_LEN
    return ((c0[:, None] < j0[None, :] + SLC_LEN) & (c0[:, None] + CMP_LEN > j0[None, :]))


def _select(score, qpos, ns):
    tblk = qpos // SLC_LEN
    j = jnp.arange(ns)
    forced = (j[None, :] == 0) | (j[None, :] == tblk[:, None]) | (j[None, :] == tblk[:, None] - 1)
    score = jnp.where(forced, jnp.inf, jnp.where(j[None, :] > tblk[:, None], -jnp.inf, score))
    _, idx = lax.top_k(score, min(N_SEL, ns))
    return idx


def _cmp_attention_s(q, kc, vc, qpos):
    nc = kc.shape[1]
    cend = jnp.arange(nc) * CMP_STRIDE + (CMP_LEN - 1)
    mask = cend[None, :] <= qpos[:, None]
    s = jnp.einsum('bqghd,bcgd->bghqc', q, kc, preferred_element_type=f32) * (NSA_HD ** -0.5)
    p = _masked_softmax(s, mask)
    o = jnp.einsum('bghqc,bcgd->bqghd', p, vc)
    return o, p


def _attend_s(q, k, v, mask):
    s = jnp.einsum('bqghd,bkgd->bghqk', q, k, preferred_element_type=f32) * (NSA_HD ** -0.5)
    p = _masked_softmax(s, mask[None, None, None, :, :])
    return jnp.einsum('bghqk,bkgd->bqghd', p, v)


def _nsa_sample(q, qr, pos, rows, win_keys, kc, vc, cache_kv, page_table):
    DB, T = q.shape[:2]
    past = page_table.shape[1] * PAGE_SIZE
    sub = PAGE_SIZE // SLC_LEN
    npb = past // SLC_LEN
    ns = -(-(past + T) // SLC_LEN)
    nt = ns - npb
    tail = jnp.pad(rows, ((0, 0), (0, nt * SLC_LEN - T), (0, 0), (0, 0), (0, 0)))
    bi4 = jnp.arange(DB)[:, None, None, None]
    bi = bi4[..., None]
    gi = jnp.arange(NSA_KV)[None, :, None, None, None]
    ar = jnp.arange(SLC_LEN)
    o_cmp, p_cmp = _cmp_attention_s(q, kc, vc, pos)
    nc = p_cmp.shape[-1]
    score = jnp.einsum('bghqc,cj->bgqj', p_cmp, _overlap(nc, ns).astype(f32))
    sel = _select(score, pos, ns)
    jp = jnp.clip(sel, 0, npb - 1)
    phys = page_table[bi4, jp // sub][..., None]
    rp = (jp % sub)[..., None] * SLC_LEN + ar
    rt = jnp.clip(sel - npb, 0, nt - 1)[..., None] * SLC_LEN + ar
    in_past = (sel < npb)[..., None, None]
    k_sel = jnp.where(in_past, cache_kv[phys, rp, 2, gi], tail[bi, rt, 2, gi])
    v_sel = jnp.where(in_past, cache_kv[phys, rp, 3, gi], tail[bi, rt, 3, gi])
    n, L = sel.shape[-1], SLC_LEN
    kpos = sel[..., None] * SLC_LEN + jnp.arange(SLC_LEN)
    mask = (kpos <= pos[None, None, :, None, None]).reshape(DB, NSA_KV, 1, T, n * L)
    s = jnp.einsum('bqghd,bgqnkd->bghqnk', qr, k_sel, preferred_element_type=f32).reshape(DB, NSA_KV, NSA_HPG, T, n * L)
    p = _masked_softmax(s * (NSA_HD ** -0.5), mask).reshape(DB, NSA_KV, NSA_HPG, T, n, L)
    o_slc = jnp.einsum('bghqnk,bgqnkd->bqghd', p, v_sel)
    wb = win_keys.shape[1] - T
    kpos_w = past - wb + jnp.arange(wb + T)
    wmask = (kpos_w[None, :] <= pos[:, None]) & (kpos_w[None, :] >= pos[:, None] - WINDOW)
    o_win = _attend_s(qr, win_keys[:, :, 0], win_keys[:, :, 1], wmask)
    return o_cmp, o_slc, o_win


def kernel(x_prompt, x_sample, cache_kv, cache_win, state_ret, page_table,
           w_ret_in, w_ret_out, w_kv, w_cmp_k1, w_cmp_k2, pe_cmp_k, w_cmp_v1, w_cmp_v2, pe_cmp_v,
           w_nsa_in, w_nsa_out, w_router, b_router, w_exp_gate, w_exp_up, w_exp_down, ln_gain, ln_bias):
    B, S, D = x_prompt.shape
    DB, T, _ = x_sample.shape
    past = page_table.shape[1] * PAGE_SIZE
    wb = cache_win.shape[1]
    G, HPG, hd = NSA_KV, NSA_HPG, NSA_HD
    P = B * S
    Ttot = P + DB * TP
    pos_p = jnp.arange(S, dtype=jnp.int32)
    pos_sp = past + jnp.arange(TP, dtype=jnp.int32)
    pos_s = pos_sp[:T]
    pos_flat = jnp.concatenate([jnp.tile(pos_p, B), jnp.tile(pos_sp, DB)])
    chunk = min(RET_CHUNK, S)

    xs_pad = jnp.pad(x_sample, ((0, 0), (0, TP - T), (0, 0)))
    xt = jnp.concatenate([x_prompt.reshape(P, D), xs_pad.reshape(DB * TP, D)], axis=0)

    ret_p, ret_s = [], []
    for l in range(DEPTH):
        if l < N_A_LAYERS:
            proj = _mm(xt, w_ret_in, layer=l)
            s0 = jnp.zeros((B, RET_HEADS, RET_DK, RET_DV), f32)
            o, sp = _retention(proj, pos_p[:chunk * (S // chunk)], s0, B, S // chunk, chunk, float(chunk), 0, None)
            o, ss = _retention(proj, pos_sp, state_ret[l].astype(f32), DB, 1, TP, float(T), P, o)
            ret_p.append(sp.astype(state_ret.dtype))
            ret_s.append(ss.astype(state_ret.dtype))
            y = _mm(o, w_ret_out, layer=l, tn=512)
        else:
            if l == N_A_LAYERS:
                kv = _mm(xt, w_kv)
                gd = G * hd
                rows = jnp.concatenate([kv[:, :2 * gd], _rope_heads(kv[:, 2 * gd:3 * gd], pos_flat),
                                        kv[:, 3 * gd:4 * gd]], axis=1)
                win_rows = jnp.concatenate([_rope_heads(kv[:, 4 * gd:5 * gd], pos_flat), kv[:, 5 * gd:]], axis=1)
                rows_p = rows[:P].reshape(B, S, 4, G, hd)
                rows_s = rows[P:].reshape(DB, TP, 4, G, hd)[:, :T]
                win_rows_s = win_rows[P:].reshape(DB, TP, 2, G, hd)[:, :T]
                win_keys_s = jnp.concatenate([cache_win, win_rows_s.astype(cache_win.dtype)], axis=1)

                def chunks_of(r):
                    n, N = r.shape[:2]
                    nch = N // CMP_STRIDE
                    c = r[:, :nch * CMP_STRIDE].reshape(n, nch, CMP_STRIDE, G, hd)
                    return c.transpose(0, 3, 1, 2, 4).reshape(n * G, nch, CMP_STRIDE * hd)

                def comp(r, w1, w2, pe):
                    n = r.shape[0]
                    c = _compress(chunks_of(r), w1, w2, pe)
                    return c.reshape(n, G, c.shape[1], hd).transpose(0, 2, 1, 3)

                kc_p = comp(rows_p[:, :, 0], w_cmp_k1, w_cmp_k2, pe_cmp_k)
                vc_p = comp(rows_p[:, :, 1], w_cmp_v1, w_cmp_v2, pe_cmp_v)
                past_cmp = cache_kv[page_table, :, :2].reshape(DB, past, 2, G, hd)
                kc_s = comp(jnp.concatenate([past_cmp[:, :, 0], rows_s[:, :, 0]], axis=1), w_cmp_k1, w_cmp_k2, pe_cmp_k)
                vc_s = comp(jnp.concatenate([past_cmp[:, :, 1], rows_s[:, :, 1]], axis=1), w_cmp_v1, w_cmp_v2, pe_cmp_v)
                nc_p = kc_p.shape[1]
                ncp = -(-nc_p // 128) * 128
                ns_p = -(-S // SLC_LEN)

                def pad_c(c):
                    c = jnp.pad(c, ((0, 0), (0, ncp - nc_p), (0, 0), (0, 0)))
                    return c.transpose(0, 2, 1, 3).reshape(B * G, ncp, hd)

                kc_pp, vc_pp = pad_c(kc_p), pad_c(vc_p)
                ov_p = _overlap(ncp, ns_p).astype(bf16)
            li = l - N_A_LAYERS
            hq = NSA_HEADS * hd
            w_in = w_nsa_in[li]
            q = _mm(xt, w_in[:, :hq])
            gates = jax.nn.sigmoid(_mm(xt, w_in[:, hq:]))
            qr = _rope_heads(q, pos_flat)
            o_cmp, score = _cmp_prompt(q, kc_pp, vc_pp, ov_p, B, S)
            sel = _select(score.reshape(B, G, S, ns_p), pos_p, ns_p)
            selmask = jnp.sum(jax.nn.one_hot(sel, ns_p, dtype=f32), axis=-2).reshape(B * G, S, ns_p)
            o_slc, o_win = _slc_win_prompt(qr, selmask, rows, win_rows, B, S)
            q_s = q[P:].reshape(DB, TP, G, HPG, hd)[:, :T]
            qr_s = qr[P:].reshape(DB, TP, G, HPG, hd)[:, :T]
            oc_s, os_s, ow_s = _nsa_sample(q_s, qr_s, pos_s, rows_s, win_keys_s, kc_s, vc_s, cache_kv, page_table)

            def put(o_flat, o_s):
                o_s = jnp.pad(o_s.reshape(DB, T, hq), ((0, 0), (0, TP - T), (0, 0))).reshape(DB * TP, hq)
                return lax.dynamic_update_slice(o_flat, o_s, (P, 0))

            g3 = gates.reshape(Ttot, NSA_HEADS, 3)
            merged = (g3[..., 0:1] * put(o_cmp, oc_s).reshape(Ttot, NSA_HEADS, hd)
                      + g3[..., 1:2] * put(o_slc, os_s).reshape(Ttot, NSA_HEADS, hd)
                      + g3[..., 2:3] * put(o_win, ow_s).reshape(Ttot, NSA_HEADS, hd)).reshape(Ttot, hq)
            y = _mm(merged, w_nsa_out, layer=li)
        xt = _post_norm(xt, y, ln_gain[l, 0], ln_bias[l, 0])
        yt = _moe(xt, w_router, b_router, w_exp_gate, w_exp_up, w_exp_down, l)
        xt = _post_norm(xt, yt, ln_gain[l, 1], ln_bias[l, 1])

    xp = xt[:P].reshape(B, S, D)
    xs = xt[P:].reshape(DB, TP, D)[:, :T]
    win_rows_p = win_rows[:P].reshape(B, S, 2, G, hd)
    if S >= wb:
        win_p = win_rows_p[:, S - wb:]
    else:
        win_p = jnp.pad(win_rows_p, ((0, 0), (wb - S, 0), (0, 0), (0, 0), (0, 0)))
    win_s = win_keys_s[:, T:]
    return (xp, xs, rows_p, rows_s, win_p, win_s, jnp.stack(ret_p), jnp.stack(ret_s))
```

```python
import functools

import jax
import jax.numpy as jnp
from jax import lax
from jax.experimental import pallas as pl
from jax.experimental.pallas import tpu as pltpu

f32 = jnp.float32
bf16 = jnp.bfloat16

D_MODEL = 2048
DEPTH = 4
PAGE_SIZE = 128
N_A_LAYERS = DEPTH // 2
RET_HEADS = 8
RET_DK = D_MODEL // RET_HEADS
RET_DV = 2 * RET_DK
RET_CHUNK = 128
RET_THETA = 10000.0
NSA_HEADS = 16
NSA_HD = D_MODEL // NSA_HEADS
NSA_KV = 4
NSA_HPG = NSA_HEADS // NSA_KV
CMP_LEN = 32
CMP_STRIDE = 16
SLC_LEN = 64
N_SEL = 16
WINDOW = 512
QBLK = 128
ROPE_THETA = 500000.0
ROT_DIM = NSA_HD // 4
N_KV_PROJ = 6
N_EXPERTS = 16
N_GROUPS = 4
EXPERTS_PER_GROUP = N_EXPERTS // N_GROUPS
TOPK_GROUPS = 1
TOP_K = 2
D_EXPERT = 1408
DN_ALPHA = (2.0 * DEPTH) ** 0.25
LN_EPS = 1e-5

TP = 16
MOE_TM = 512
MOE_TF = 128
VMEM_LIMIT = 56 * 1024 * 1024
NEG = -1e30


def _cparams(sem):
    return pltpu.CompilerParams(dimension_semantics=sem, vmem_limit_bytes=VMEM_LIMIT)


def _bdot(a, b):
    return jnp.dot(a.astype(bf16), b.astype(bf16), preferred_element_type=f32)


def _bdot_nt(a, b):
    return lax.dot_general(a.astype(bf16), b.astype(bf16), (((1,), (1,)), ((), ())), preferred_element_type=f32)


def _mm_kernel(x_ref, w_ref, o_ref):
    o_ref[...] = _bdot(x_ref[...], w_ref[...])


def _mm3_kernel(x_ref, w_ref, o_ref):
    x = x_ref[...]
    w = w_ref[...]
    xh = x.astype(bf16)
    wh = w.astype(bf16)
    xl = (x - xh.astype(f32)).astype(bf16)
    wl = (w - wh.astype(f32)).astype(bf16)
    acc = jnp.dot(xh, wh, preferred_element_type=f32)
    acc += jnp.dot(xh, wl, preferred_element_type=f32)
    acc += jnp.dot(xl, wh, preferred_element_type=f32)
    o_ref[...] = acc


def _row_tile(m, pref=640):
    for t in (pref, 512, 256, 128):
        if m % t == 0:
            return t
    return min(512, m)


def _mm(x, w, layer=None, tn=1024, body=_mm_kernel):
    M, K = x.shape
    N = w.shape[-1]
    tn = min(tn, N)
    tm = _row_tile(M)
    if layer is None:
        w_spec = pl.BlockSpec((K, tn), lambda j, i: (0, j))
    else:
        w_spec = pl.BlockSpec((None, K, tn), lambda j, i: (layer, 0, j))
    return pl.pallas_call(
        body,
        grid=(pl.cdiv(N, tn), pl.cdiv(M, tm)),
        in_specs=[pl.BlockSpec((tm, K), lambda j, i: (i, 0)), w_spec],
        out_specs=pl.BlockSpec((tm, tn), lambda j, i: (i, j)),
        out_shape=jax.ShapeDtypeStruct((M, N), f32),
        compiler_params=_cparams(("parallel", "parallel")),
        name="mm",
    )(x, w)


def _pn_kernel(x_ref, y_ref, g_ref, b_ref, o_ref):
    h = DN_ALPHA * x_ref[...] + y_ref[...]
    mu = jnp.mean(h, -1, keepdims=True)
    d = h - mu
    var = jnp.mean(d * d, -1, keepdims=True)
    o_ref[...] = d * lax.rsqrt(var + LN_EPS) * g_ref[...] + b_ref[...]


def _post_norm(x, y, g, b):
    M, D = x.shape
    tm = _row_tile(M, 128)
    row = pl.BlockSpec((tm, D), lambda i: (i, 0))
    vec = pl.BlockSpec((1, D), lambda i: (0, 0))
    return pl.pallas_call(
        _pn_kernel,
        grid=(pl.cdiv(M, tm),),
        in_specs=[row, row, vec, vec],
        out_specs=row,
        out_shape=jax.ShapeDtypeStruct((M, D), f32),
        compiler_params=_cparams(("parallel",)),
        name="post_norm",
    )(x, y, g.reshape(1, D), b.reshape(1, D))


def _ret_kernel(q_ref, k_ref, v_ref, g_ref, cos_ref, sin_ref, intra_ref, qd_ref, kd_ref, cd_ref, s0_ref,
                o_ref, sfin_ref, s_scr, *, nc):
    c = pl.program_id(2)

    @pl.when(c == 0)
    def _():
        s_scr[...] = s0_ref[...]

    half = RET_DK // 2
    cos = cos_ref[...]
    sin = sin_ref[...]

    def rot(t):
        t1, t2 = t[:, :half], t[:, half:]
        return jnp.concatenate([t1 * cos - t2 * sin, t2 * cos + t1 * sin], axis=-1)

    q = rot(q_ref[...])
    k = rot(k_ref[...]) * (RET_DK ** -0.5)
    v = v_ref[...].astype(bf16)
    s = s_scr[...]
    att = _bdot_nt(q, k) * intra_ref[...]
    o = _bdot(att, v) + _bdot(q * qd_ref[...], s)
    kd = (k * kd_ref[...]).T
    s_new = s * cd_ref[...] + _bdot(kd, v)
    s_scr[...] = s_new
    mu = jnp.mean(o, -1, keepdims=True)
    d = o - mu
    var = jnp.mean(d * d, -1, keepdims=True)
    gate = g_ref[...]
    o_ref[...] = gate * jax.nn.sigmoid(gate) * (d * lax.rsqrt(var + LN_EPS))

    @pl.when(c == nc - 1)
    def _():
        sfin_ref[...] = s_new


def _ret_decay(chunk, valid):
    log_g = jnp.log(1.0 - 2.0 ** (-5.0 - jnp.arange(RET_HEADS, dtype=f32)))
    i = jnp.arange(chunk, dtype=f32)
    real = i < valid
    diff = i[:, None] - i[None, :]
    intra = jnp.where((diff >= 0) & real[None, :], jnp.exp(log_g[:, None, None] * jnp.maximum(diff, 0.0)), 0.0)
    q_dec = jnp.exp(log_g[:, None] * (i + 1.0))[:, :, None]
    k_dec = jnp.where(real, jnp.exp(log_g[:, None] * jnp.maximum(valid - 1.0 - i, 0.0)), 0.0)[:, :, None]
    c_dec = jnp.exp(log_g * valid)[:, None, None]
    return intra, q_dec, k_dec, c_dec


def _ret_tables(pos):
    inv = 1.0 / (RET_THETA ** jnp.linspace(0.0, 1.0, RET_DK // 2, dtype=f32))
    ang = pos.astype(f32)[:, None] * inv[None, :]
    return jnp.cos(ang), jnp.sin(ang)


def _retention(proj, pos, s0, nb, nc, chunk, valid, row0, o_prev):
    Ttot = proj.shape[0]
    hk, hv = RET_HEADS * RET_DK, RET_HEADS * RET_DV
    rb0 = row0 // chunk
    cos, sin = _ret_tables(pos)
    intra, q_dec, k_dec, c_dec = _ret_decay(chunk, valid)
    H = RET_HEADS

    def rowblk(b, h, c):
        return rb0 + b * nc + c

    in_specs = [
        pl.BlockSpec((chunk, RET_DK), lambda b, h, c: (rowblk(b, h, c), h)),
        pl.BlockSpec((chunk, RET_DK), lambda b, h, c: (rowblk(b, h, c), H + h)),
        pl.BlockSpec((chunk, RET_DV), lambda b, h, c: (rowblk(b, h, c), (2 * hk) // RET_DV + h)),
        pl.BlockSpec((chunk, RET_DV), lambda b, h, c: (rowblk(b, h, c), (2 * hk + hv) // RET_DV + h)),
        pl.BlockSpec((chunk, RET_DK // 2), lambda b, h, c: (c, 0)),
        pl.BlockSpec((chunk, RET_DK // 2), lambda b, h, c: (c, 0)),
        pl.BlockSpec((None, chunk, chunk), lambda b, h, c: (h, 0, 0)),
        pl.BlockSpec((None, chunk, 1), lambda b, h, c: (h, 0, 0)),
        pl.BlockSpec((None, chunk, 1), lambda b, h, c: (h, 0, 0)),
        pl.BlockSpec((None, 1, 1), lambda b, h, c: (h, 0, 0)),
        pl.BlockSpec((None, None, RET_DK, RET_DV), lambda b, h, c: (b, h, 0, 0)),
    ]
    args = [proj, proj, proj, proj, cos, sin, intra, q_dec, k_dec, c_dec, s0]
    aliases = {}
    if o_prev is not None:
        in_specs.append(pl.BlockSpec(memory_space=pl.ANY))
        args.append(o_prev)
        aliases = {len(args) - 1: 0}

    def body(*refs):
        if o_prev is not None:
            refs = refs[:11] + refs[12:]
        _ret_kernel(*refs, nc=nc)

    return pl.pallas_call(
        body,
        grid=(nb, H, nc),
        in_specs=in_specs,
        out_specs=[
            pl.BlockSpec((chunk, RET_DV), lambda b, h, c: (rowblk(b, h, c), h)),
            pl.BlockSpec((None, None, RET_DK, RET_DV), lambda b, h, c: (b, h, 0, 0)),
        ],
        out_shape=[jax.ShapeDtypeStruct((Ttot, hv), f32), jax.ShapeDtypeStruct((nb, H, RET_DK, RET_DV), f32)],
        scratch_shapes=[pltpu.VMEM((RET_DK, RET_DV), f32)],
        input_output_aliases=aliases,
        compiler_params=_cparams(("parallel", "parallel", "arbitrary")),
        name="retention",
    )(*args)


def _moe_kernel(be_ref, nu_ref, x_ref, wg_ref, wu_ref, wd_ref, o_ref, xb_scr):
    i = pl.program_id(0)
    j = pl.program_id(1)

    @pl.when(i < nu_ref[0])
    def _():
        @pl.when(j == 0)
        def _():
            xb_scr[...] = x_ref[...].astype(bf16)

        xb = xb_scr[...]
        g = jnp.dot(xb, wg_ref[...].astype(bf16), preferred_element_type=f32)
        u = jnp.dot(xb, wu_ref[...].astype(bf16), preferred_element_type=f32)
        h = g * jax.nn.sigmoid(g) * u
        y = _bdot(h, wd_ref[...])

        @pl.when(j == 0)
        def _():
            o_ref[...] = y

        @pl.when(j > 0)
        def _():
            o_ref[...] += y


def _moe_experts(xs, blk_e, n_used, w_g, w_u, w_d, layer):
    R, D = xs.shape
    n_blk = R // MOE_TM
    nf = D_EXPERT // MOE_TF

    def ii(i, nu):
        return jnp.minimum(i, nu[0] - 1)

    def jj(i, j, nu):
        return jnp.where(i < nu[0], j, nf - 1)

    grid_spec = pltpu.PrefetchScalarGridSpec(
        num_scalar_prefetch=2,
        grid=(n_blk, nf),
        in_specs=[
            pl.BlockSpec((MOE_TM, D), lambda i, j, be, nu: (ii(i, nu), 0)),
            pl.BlockSpec((None, None, D, MOE_TF), lambda i, j, be, nu: (layer, be[ii(i, nu)], 0, jj(i, j, nu))),
            pl.BlockSpec((None, None, D, MOE_TF), lambda i, j, be, nu: (layer, be[ii(i, nu)], 0, jj(i, j, nu))),
            pl.BlockSpec((None, None, MOE_TF, D), lambda i, j, be, nu: (layer, be[ii(i, nu)], jj(i, j, nu), 0)),
        ],
        out_specs=pl.BlockSpec((MOE_TM, D), lambda i, j, be, nu: (ii(i, nu), 0)),
        scratch_shapes=[pltpu.VMEM((MOE_TM, D), bf16)],
    )
    return pl.pallas_call(
        _moe_kernel,
        grid_spec=grid_spec,
        out_shape=jax.ShapeDtypeStruct((R, D), f32),
        compiler_params=_cparams(("arbitrary", "arbitrary")),
        name="moe_experts",
    )(blk_e, n_used, xs, w_g, w_u, w_d)


def _topk_small(x, k):
    lane = jnp.arange(x.shape[-1])
    vals, idxs = [], []
    for _ in range(k):
        i = jnp.argmax(x, axis=-1)
        vals.append(jnp.max(x, axis=-1))
        idxs.append(i)
        x = jnp.where(lane == i[..., None], -jnp.inf, x)
    return jnp.stack(vals, -1), jnp.stack(idxs, -1)


def _moe(x, w_router, b_router, w_g, w_u, w_d, layer):
    T, D = x.shape
    logits = _mm(x, w_router, body=_mm3_kernel)
    aff = jax.nn.sigmoid(logits)
    biased = aff + b_router.astype(f32)
    grp = biased.reshape(T, N_GROUPS, EXPERTS_PER_GROUP)
    grp_score = _topk_small(grp, 2)[0].sum(-1)
    _, top_g = _topk_small(grp_score, TOPK_GROUPS)
    g_mask = jnp.any(top_g[:, :, None] == jnp.arange(N_GROUPS)[None, None, :], axis=1)
    e_mask = jnp.repeat(g_mask, EXPERTS_PER_GROUP, axis=1)
    _, idx = _topk_small(jnp.where(e_mask, biased, -jnp.inf), TOP_K)
    w = jnp.take_along_axis(aff, idx, axis=1)
    w = w / jnp.sum(w, -1, keepdims=True)
    A = T * TOP_K
    e_flat = idx.reshape(A).astype(jnp.int32)
    order = jnp.argsort(e_flat)
    e_s = e_flat[order]
    counts = jnp.bincount(e_flat, length=N_EXPERTS)
    start = jnp.cumsum(counts) - counts
    padded = (counts + MOE_TM - 1) // MOE_TM * MOE_TM
    pend = jnp.cumsum(padded)
    pstart = pend - padded
    dest = (pstart[e_s] + (jnp.arange(A) - start[e_s])).astype(jnp.int32)
    n_blk = -(-A // MOE_TM) + N_EXPERTS
    rows_src = jnp.full((n_blk * MOE_TM,), T, jnp.int32).at[dest].set((order // TOP_K).astype(jnp.int32))
    xs = jnp.concatenate([x, jnp.zeros((1, D), x.dtype)], axis=0)[rows_src]
    blk_e = jnp.minimum(jnp.searchsorted(pend, jnp.arange(n_blk) * MOE_TM, side='right'), N_EXPERTS - 1)
    n_used = (pend[-1] // MOE_TM).astype(jnp.int32).reshape(1)
    out = _moe_experts(xs, blk_e.astype(jnp.int32), n_used, w_g, w_u, w_d, layer)
    pos = jnp.zeros((A,), jnp.int32).at[order].set(dest).reshape(T, TOP_K)
    return out[pos[:, 0]] * w[:, 0:1] + out[pos[:, 1]] * w[:, 1:2]


def _nsa_inv_freq():
    return ROPE_THETA ** (-jnp.arange(0, ROT_DIM, 2, dtype=f32) / ROT_DIM)


def _rope_heads(x, pos):
    T = x.shape[0]
    half = ROT_DIM // 2
    ang = pos.astype(f32)[:, None] * _nsa_inv_freq()[None, :]
    cos = jnp.cos(ang)[:, None, :]
    sin = jnp.sin(ang)[:, None, :]
    xh = x.reshape(T, -1, NSA_HD)
    x1, x2 = xh[..., :half], xh[..., half:ROT_DIM]
    out = jnp.concatenate([x1 * cos - x2 * sin, x2 * cos + x1 * sin, xh[..., ROT_DIM:]], axis=-1)
    return out.reshape(x.shape)


def _masked_softmax(s, mask):
    s = jnp.where(mask, s, NEG)
    m = jnp.max(s, axis=-1, keepdims=True)
    m = jnp.where(m <= 0.5 * NEG, 0.0, m)
    e = jnp.where(mask, jnp.exp(s - m), 0.0)
    den = jnp.sum(e, axis=-1, keepdims=True)
    return e / jnp.where(den > 0, den, 1.0)


def _compress(chunks, w1, w2, pe):
    n, nchunk, kk = chunks.shape
    wab = jnp.concatenate([w1[:kk], w1[kk:]], axis=1)
    y = _mm(chunks.reshape(n * nchunk, kk), wab).reshape(n, nchunk, 2 * NSA_HD)
    bias = _mm(jnp.broadcast_to(pe.reshape(1, CMP_LEN * NSA_HD), (8, CMP_LEN * NSA_HD)), w1)[:1]
    pre = y[:, :-1, :NSA_HD] + y[:, 1:, NSA_HD:] + bias
    nc = nchunk - 1
    return _mm(jax.nn.silu(pre).reshape(n * nc, NSA_HD), w2).reshape(n, nc, NSA_HD)


def _cmp_kernel(q_ref, kc_ref, vc_ref, ov_ref, o_ref, sc_ref):
    qi = pl.program_id(2)
    ncp = kc_ref.shape[0]
    qpos = qi * QBLK + lax.broadcasted_iota(jnp.int32, (QBLK, ncp), 0)
    cend = lax.broadcasted_iota(jnp.int32, (QBLK, ncp), 1) * CMP_STRIDE + (CMP_LEN - 1)
    mask = cend <= qpos
    kc = kc_ref[...].astype(bf16)
    vc = vc_ref[...].astype(bf16)
    ov = ov_ref[...]
    score = jnp.zeros((QBLK, ov.shape[1]), f32)
    for h in range(NSA_HPG):
        q = q_ref[:, h * NSA_HD:(h + 1) * NSA_HD]
        s = _bdot_nt(q, kc) * (NSA_HD ** -0.5)
        p = _masked_softmax(s, mask).astype(bf16)
        o_ref[:, h * NSA_HD:(h + 1) * NSA_HD] = jnp.dot(p, vc, preferred_element_type=f32)
        score += jnp.dot(p, ov, preferred_element_type=f32)
    sc_ref[...] = score


def _cmp_prompt(q, kc, vc, overlap, B, S):
    Ttot = q.shape[0]
    nq = S // QBLK
    ncp = kc.shape[1]
    ns = overlap.shape[1]
    G = NSA_KV
    gw = NSA_HPG * NSA_HD
    return pl.pallas_call(
        _cmp_kernel,
        grid=(B, G, nq),
        in_specs=[
            pl.BlockSpec((QBLK, gw), lambda b, g, i: (b * nq + i, g)),
            pl.BlockSpec((None, ncp, NSA_HD), lambda b, g, i: (b * G + g, 0, 0)),
            pl.BlockSpec((None, ncp, NSA_HD), lambda b, g, i: (b * G + g, 0, 0)),
            pl.BlockSpec((ncp, ns), lambda b, g, i: (0, 0)),
        ],
        out_specs=[
            pl.BlockSpec((QBLK, gw), lambda b, g, i: (b * nq + i, g)),
            pl.BlockSpec((None, QBLK, ns), lambda b, g, i: (b * G + g, i, 0)),
        ],
        out_shape=[jax.ShapeDtypeStruct((Ttot, G * gw), f32), jax.ShapeDtypeStruct((B * G, S, ns), f32)],
        compiler_params=_cparams(("parallel", "parallel", "parallel")),
        name="cmp_prompt",
    )(q, kc, vc, overlap)


SEL_TK = 512


LOG2E = 1.4426950408889634
LANES = 128


def _fold_lanes(x, op):
    parts = [x[:, j * LANES:(j + 1) * LANES] for j in range(x.shape[1] // LANES)]
    return functools.reduce(op, parts)


def _slc_win_kernel(q_ref, sel_ref, ks_ref, vs_ref, kw_ref, vw_ref, os_ref, ow_ref, s_scr, m_scr, l_scr, acc_scr):
    qi = pl.program_id(2)
    c2 = (NSA_HD ** -0.5) * LOG2E
    qs = [(q_ref[:, h * NSA_HD:(h + 1) * NSA_HD] * c2).astype(bf16) for h in range(NSA_HPG)]
    sel = sel_ref[...].astype(bf16)
    ns = sel.shape[1]
    bpt = SEL_TK // SLC_LEN
    nl = SEL_TK // LANES

    m_scr[...] = jnp.full(m_scr.shape, NEG, f32)
    l_scr[...] = jnp.zeros(l_scr.shape, f32)
    acc_scr[...] = jnp.zeros(acc_scr.shape, f32)
    qpos = qi * QBLK + lax.broadcasted_iota(jnp.int32, (QBLK, SEL_TK), 0)
    n_tiles = (qi * QBLK + QBLK - 1) // SEL_TK + 1

    def scores(t, carry):
        k0 = pl.multiple_of(t * SEL_TK, SEL_TK)
        k = ks_ref[pl.ds(k0, SEL_TK), :].astype(bf16)
        blk = lax.broadcasted_iota(jnp.int32, (ns, SEL_TK), 1) // SLC_LEN + t * bpt
        expand = jnp.where(blk == lax.broadcasted_iota(jnp.int32, (ns, SEL_TK), 0), 1.0, 0.0).astype(bf16)
        chosen = jnp.dot(sel, expand, preferred_element_type=f32) > 0.5
        kpos = k0 + lax.broadcasted_iota(jnp.int32, (QBLK, SEL_TK), 1)
        bias = jnp.where(chosen & (kpos <= qpos), 0.0, NEG)
        for h in range(NSA_HPG):
            s = _bdot_nt(qs[h], k) + bias
            s_scr[h, t] = s
            m_scr[h] = jnp.maximum(m_scr[h], _fold_lanes(s, jnp.maximum))
        return carry

    lax.fori_loop(0, n_tiles, scores, 0)
    for h in range(NSA_HPG):
        m_scr[h] = jnp.broadcast_to(jnp.max(m_scr[h], axis=-1, keepdims=True), (QBLK, LANES))

    def accumulate(t, carry):
        k0 = pl.multiple_of(t * SEL_TK, SEL_TK)
        v = vs_ref[pl.ds(k0, SEL_TK), :].astype(bf16)
        for h in range(NSA_HPG):
            s = s_scr[h, t]
            m = m_scr[h]
            p = jnp.concatenate([jnp.exp2(s[:, j * LANES:(j + 1) * LANES] - m) for j in range(nl)], axis=-1)
            l_scr[h] += _fold_lanes(p, jnp.add)
            acc_scr[h] += jnp.dot(p.astype(bf16), v, preferred_element_type=f32)
        return carry

    lax.fori_loop(0, n_tiles, accumulate, 0)
    for h in range(NSA_HPG):
        l = jnp.sum(l_scr[h], axis=-1, keepdims=True)
        os_ref[:, h * NSA_HD:(h + 1) * NSA_HD] = acc_scr[h] / l

    wk = WINDOW + QBLK
    w0 = pl.multiple_of(jnp.maximum(qi * QBLK - WINDOW, 0), QBLK)
    kw = kw_ref[pl.ds(w0, wk), :].astype(bf16)
    vw = vw_ref[pl.ds(w0, wk), :].astype(bf16)
    qp = qi * QBLK + lax.broadcasted_iota(jnp.int32, (QBLK, wk), 0)
    kp = w0 + lax.broadcasted_iota(jnp.int32, (QBLK, wk), 1)
    wbias = jnp.where((kp <= qp) & (kp >= qp - WINDOW), 0.0, NEG)
    for h in range(NSA_HPG):
        s = _bdot_nt(qs[h], kw) + wbias
        p = jnp.exp2(s - jnp.max(s, axis=-1, keepdims=True))
        o = jnp.dot(p.astype(bf16), vw, preferred_element_type=f32)
        ow_ref[:, h * NSA_HD:(h + 1) * NSA_HD] = o / jnp.sum(p, axis=-1, keepdims=True)


def _slc_win_prompt(qr, selmask, rows, win_rows, B, S):
    Ttot = qr.shape[0]
    nq = S // QBLK
    ns = selmask.shape[-1]
    G = NSA_KV
    gw = NSA_HPG * NSA_HD
    qspec = pl.BlockSpec((QBLK, gw), lambda b, g, i: (b * nq + i, g))
    return pl.pallas_call(
        _slc_win_kernel,
        grid=(B, G, nq),
        in_specs=[
            qspec,
            pl.BlockSpec((None, QBLK, ns), lambda b, g, i: (b * G + g, i, 0)),
            pl.BlockSpec((S, NSA_HD), lambda b, g, i: (b, 2 * G + g)),
            pl.BlockSpec((S, NSA_HD), lambda b, g, i: (b, 3 * G + g)),
            pl.BlockSpec((S, NSA_HD), lambda b, g, i: (b, g)),
            pl.BlockSpec((S, NSA_HD), lambda b, g, i: (b, G + g)),
        ],
        out_specs=[qspec, qspec],
        out_shape=[jax.ShapeDtypeStruct((Ttot, G * gw), f32)] * 2,
        scratch_shapes=[
            pltpu.VMEM((NSA_HPG, S // SEL_TK, QBLK, SEL_TK), f32),
            pltpu.VMEM((NSA_HPG, QBLK, LANES), f32),
            pltpu.VMEM((NSA_HPG, QBLK, LANES), f32),
            pltpu.VMEM((NSA_HPG, QBLK, NSA_HD), f32),
        ],
        compiler_params=_cparams(("parallel", "parallel", "arbitrary")),
        name="slc_win_prompt",
    )(qr, selmask, rows, rows, win_rows, win_rows)


def _overlap(nc, ns):
    c0 = jnp.arange(nc) * CMP_STRIDE
    j0 = jnp.arange(ns) * SLC_LEN
    return ((c0[:, None] < j0[None, :] + SLC_LEN) & (c0[:, None] + CMP_LEN > j0[None, :]))


def _select(score, qpos, ns):
    tblk = qpos // SLC_LEN
    j = jnp.arange(ns)
    forced = (j[None, :] == 0) | (j[None, :] == tblk[:, None]) | (j[None, :] == tblk[:, None] - 1)
    score = jnp.where(forced, jnp.inf, jnp.where(j[None, :] > tblk[:, None], -jnp.inf, score))
    _, idx = lax.top_k(score, min(N_SEL, ns))
    return idx


def _cmp_attention_s(q, kc, vc, qpos):
    nc = kc.shape[1]
    cend = jnp.arange(nc) * CMP_STRIDE + (CMP_LEN - 1)
    mask = cend[None, :] <= qpos[:, None]
    s = jnp.einsum('bqghd,bcgd->bghqc', q, kc, preferred_element_type=f32) * (NSA_HD ** -0.5)
    p = _masked_softmax(s, mask)
    o = jnp.einsum('bghqc,bcgd->bqghd', p, vc)
    return o, p


def _attend_s(q, k, v, mask):
    s = jnp.einsum('bqghd,bkgd->bghqk', q, k, preferred_element_type=f32) * (NSA_HD ** -0.5)
    p = _masked_softmax(s, mask[None, None, None, :, :])
    return jnp.einsum('bghqk,bkgd->bqghd', p, v)


HPAD = 16


def _slc_s_kernel(sel_ref, pt_ref, tsel_ref, q_ref, tk_ref, tv_ref, *refs, n_sel, npb, nt, past, T):
    k_refs, v_refs, o_ref = refs[:n_sel], refs[n_sel:2 * n_sel], refs[2 * n_sel]
    b, g, t = pl.program_id(0), pl.program_id(1), pl.program_id(2)
    flat = (b * NSA_KV + g) * T + t
    q = q_ref[...].astype(bf16)
    qpos = past + t
    lane = lax.broadcasted_iota(jnp.int32, (HPAD, SLC_LEN), 1)
    scale = NSA_HD ** -0.5
    parts = []
    for n in range(n_sel):
        bid = sel_ref[flat * n_sel + n]
        lim = jnp.where(bid < npb, qpos - bid * SLC_LEN, -1)
        mask = lane <= lim
        parts.append((jnp.where(mask, _bdot_nt(q, k_refs[n][...]) * scale, NEG), mask, v_refs[n][...]))
    for j in range(nt):
        lim = jnp.where(tsel_ref[flat * nt + j] > 0, qpos - (npb + j) * SLC_LEN, -1)
        mask = lane <= lim
        kt = tk_ref[j * SLC_LEN:(j + 1) * SLC_LEN, :]
        parts.append((jnp.where(mask, _bdot_nt(q, kt) * scale, NEG), mask, tv_ref[j * SLC_LEN:(j + 1) * SLC_LEN, :]))
    m = functools.reduce(jnp.maximum, [jnp.max(s, axis=-1, keepdims=True) for s, _, _ in parts])
    m = jnp.where(m <= 0.5 * NEG, 0.0, m)
    den = jnp.zeros((HPAD, 1), f32)
    acc = jnp.zeros((HPAD, NSA_HD), f32)
    for s, mask, v in parts:
        e = jnp.where(mask, jnp.exp(s - m), 0.0)
        den += jnp.sum(e, axis=-1, keepdims=True)
        acc += _bdot(e, v)
    o_ref[...] = acc / jnp.where(den > 0, den, 1.0)


def _slc_sample(qr, sel, tail, cache_kv, page_table, past):
    DB, T, G, H, d = qr.shape
    n_sel = sel.shape[-1]
    sub = PAGE_SIZE // SLC_LEN
    npb = past // SLC_LEN
    nt = tail.shape[1] // SLC_LEN
    n_pages = page_table.shape[1]
    q8 = jnp.pad(qr.transpose(0, 2, 1, 3, 4), ((0, 0), (0, 0), (0, 0), (0, HPAD - H), (0, 0)))
    q8 = q8.reshape(DB * G * T, HPAD, d)
    tsel = jnp.any(sel[..., None] == npb + jnp.arange(nt), axis=-2).astype(jnp.int32)
    cache = cache_kv.reshape(cache_kv.shape[0] * sub, SLC_LEN, 4 * G * d)
    tail2 = tail.reshape(DB, nt * SLC_LEN, 4 * G * d)

    def cache_map(n, typ):
        def f(b, g, t, sel_r, pt_r, ts_r):
            bid = sel_r[((b * G + g) * T + t) * n_sel + n]
            jp = jnp.clip(bid, 0, npb - 1)
            return (pt_r[b * n_pages + jp // sub] * sub + jp % sub, 0, typ * G + g)
        return f

    in_specs = [
        pl.BlockSpec((None, HPAD, d), lambda b, g, t, *_: ((b * G + g) * T + t, 0, 0)),
        pl.BlockSpec((None, nt * SLC_LEN, d), lambda b, g, t, *_: (b, 0, 2 * G + g)),
        pl.BlockSpec((None, nt * SLC_LEN, d), lambda b, g, t, *_: (b, 0, 3 * G + g)),
    ]
    in_specs += [pl.BlockSpec((None, SLC_LEN, d), cache_map(n, 2)) for n in range(n_sel)]
    in_specs += [pl.BlockSpec((None, SLC_LEN, d), cache_map(n, 3)) for n in range(n_sel)]
    out = pl.pallas_call(
        functools.partial(_slc_s_kernel, n_sel=n_sel, npb=npb, nt=nt, past=past, T=T),
        grid_spec=pltpu.PrefetchScalarGridSpec(
            num_scalar_prefetch=3,
            grid=(DB, G, T),
            in_specs=in_specs,
            out_specs=pl.BlockSpec((None, HPAD, d), lambda b, g, t, *_: ((b * G + g) * T + t, 0, 0)),
        ),
        out_shape=jax.ShapeDtypeStruct((DB * G * T, HPAD, d), f32),
        compiler_params=_cparams(("arbitrary", "arbitrary", "arbitrary")),
        name="slc_sample",
    )(sel.reshape(-1), page_table.reshape(-1).astype(jnp.int32), tsel.reshape(-1), q8, tail2, tail2,
      *([cache] * (2 * n_sel)))
    return out.reshape(DB, G, T, HPAD, d)[:, :, :, :H].transpose(0, 2, 1, 3, 4)


def _nsa_sample(q, qr, pos, rows, win_keys, kc, vc, cache_kv, page_table):
    DB, T = q.shape[:2]
    past = page_table.shape[1] * PAGE_SIZE
    sub = PAGE_SIZE // SLC_LEN
    npb = past // SLC_LEN
    ns = -(-(past + T) // SLC_LEN)
    nt = ns - npb
    tail = jnp.pad(rows, ((0, 0), (0, nt * SLC_LEN - T), (0, 0), (0, 0), (0, 0)))
    o_cmp, p_cmp = _cmp_attention_s(q, kc, vc, pos)
    nc = p_cmp.shape[-1]
    score = jnp.einsum('bghqc,cj->bgqj', p_cmp, _overlap(nc, ns).astype(f32))
    sel = _select(score, pos, ns).astype(jnp.int32)
    o_slc = _slc_sample(qr, sel, tail, cache_kv, page_table, past)
    wb = win_keys.shape[1] - T
    kpos_w = past - wb + jnp.arange(wb + T)
    wmask = (kpos_w[None, :] <= pos[:, None]) & (kpos_w[None, :] >= pos[:, None] - WINDOW)
    o_win = _attend_s(qr, win_keys[:, :, 0], win_keys[:, :, 1], wmask)
    return o_cmp, o_slc, o_win


def kernel(x_prompt, x_sample, cache_kv, cache_win, state_ret, page_table,
           w_ret_in, w_ret_out, w_kv, w_cmp_k1, w_cmp_k2, pe_cmp_k, w_cmp_v1, w_cmp_v2, pe_cmp_v,
           w_nsa_in, w_nsa_out, w_router, b_router, w_exp_gate, w_exp_up, w_exp_down, ln_gain, ln_bias):
    B, S, D = x_prompt.shape
    DB, T, _ = x_sample.shape
    past = page_table.shape[1] * PAGE_SIZE
    wb = cache_win.shape[1]
    G, HPG, hd = NSA_KV, NSA_HPG, NSA_HD
    P = B * S
    Ttot = P + DB * TP
    pos_p = jnp.arange(S, dtype=jnp.int32)
    pos_sp = past + jnp.arange(TP, dtype=jnp.int32)
    pos_s = pos_sp[:T]
    pos_flat = jnp.concatenate([jnp.tile(pos_p, B), jnp.tile(pos_sp, DB)])
    chunk = min(RET_CHUNK, S)

    xs_pad = jnp.pad(x_sample, ((0, 0), (0, TP - T), (0, 0)))
    xt = jnp.concatenate([x_prompt.reshape(P, D), xs_pad.reshape(DB * TP, D)], axis=0)

    ret_p, ret_s = [], []
    for l in range(DEPTH):
        if l < N_A_LAYERS:
            proj = _mm(xt, w_ret_in, layer=l)
            s0 = jnp.zeros((B, RET_HEADS, RET_DK, RET_DV), f32)
            o, sp = _retention(proj, pos_p[:chunk * (S // chunk)], s0, B, S // chunk, chunk, float(chunk), 0, None)
            o, ss = _retention(proj, pos_sp, state_ret[l].astype(f32), DB, 1, TP, float(T), P, o)
            ret_p.append(sp.astype(state_ret.dtype))
            ret_s.append(ss.astype(state_ret.dtype))
            y = _mm(o, w_ret_out, layer=l, tn=512)
        else:
            if l == N_A_LAYERS:
                kv = _mm(xt, w_kv)
                gd = G * hd
                rows = jnp.concatenate([kv[:, :2 * gd], _rope_heads(kv[:, 2 * gd:3 * gd], pos_flat),
                                        kv[:, 3 * gd:4 * gd]], axis=1)
                win_rows = jnp.concatenate([_rope_heads(kv[:, 4 * gd:5 * gd], pos_flat), kv[:, 5 * gd:]], axis=1)
                rows_p = rows[:P].reshape(B, S, 4, G, hd)
                rows_s = rows[P:].reshape(DB, TP, 4, G, hd)[:, :T]
                win_rows_s = win_rows[P:].reshape(DB, TP, 2, G, hd)[:, :T]
                win_keys_s = jnp.concatenate([cache_win, win_rows_s.astype(cache_win.dtype)], axis=1)

                def chunks_of(r):
                    n, N = r.shape[:2]
                    nch = N // CMP_STRIDE
                    c = r[:, :nch * CMP_STRIDE].reshape(n, nch, CMP_STRIDE, G, hd)
                    return c.transpose(0, 3, 1, 2, 4).reshape(n * G, nch, CMP_STRIDE * hd)

                def comp(r, w1, w2, pe):
                    n = r.shape[0]
                    c = _compress(chunks_of(r), w1, w2, pe)
                    return c.reshape(n, G, c.shape[1], hd).transpose(0, 2, 1, 3)

                kc_p = comp(rows_p[:, :, 0], w_cmp_k1, w_cmp_k2, pe_cmp_k)
                vc_p = comp(rows_p[:, :, 1], w_cmp_v1, w_cmp_v2, pe_cmp_v)
                past_cmp = cache_kv[page_table, :, :2].reshape(DB, past, 2, G, hd)
                kc_s = comp(jnp.concatenate([past_cmp[:, :, 0], rows_s[:, :, 0]], axis=1), w_cmp_k1, w_cmp_k2, pe_cmp_k)
                vc_s = comp(jnp.concatenate([past_cmp[:, :, 1], rows_s[:, :, 1]], axis=1), w_cmp_v1, w_cmp_v2, pe_cmp_v)
                nc_p = kc_p.shape[1]
                ncp = -(-nc_p // 128) * 128
                ns_p = -(-S // SLC_LEN)

                def pad_c(c):
                    c = jnp.pad(c, ((0, 0), (0, ncp - nc_p), (0, 0), (0, 0)))
                    return c.transpose(0, 2, 1, 3).reshape(B * G, ncp, hd)

                kc_pp, vc_pp = pad_c(kc_p), pad_c(vc_p)
                ov_p = _overlap(ncp, ns_p).astype(bf16)
            li = l - N_A_LAYERS
            hq = NSA_HEADS * hd
            w_in = w_nsa_in[li]
            q = _mm(xt, w_in[:, :hq])
            gates = jax.nn.sigmoid(_mm(xt, w_in[:, hq:]))
            qr = _rope_heads(q, pos_flat)
            o_cmp, score = _cmp_prompt(q, kc_pp, vc_pp, ov_p, B, S)
            sel = _select(score.reshape(B, G, S, ns_p), pos_p, ns_p)
            selmask = jnp.sum(jax.nn.one_hot(sel, ns_p, dtype=f32), axis=-2).reshape(B * G, S, ns_p)
            o_slc, o_win = _slc_win_prompt(qr, selmask, rows, win_rows, B, S)
            q_s = q[P:].reshape(DB, TP, G, HPG, hd)[:, :T]
            qr_s = qr[P:].reshape(DB, TP, G, HPG, hd)[:, :T]
            oc_s, os_s, ow_s = _nsa_sample(q_s, qr_s, pos_s, rows_s, win_keys_s, kc_s, vc_s, cache_kv, page_table)

            def put(o_flat, o_s):
                o_s = jnp.pad(o_s.reshape(DB, T, hq), ((0, 0), (0, TP - T), (0, 0))).reshape(DB * TP, hq)
                return lax.dynamic_update_slice(o_flat, o_s, (P, 0))

            g3 = gates.reshape(Ttot, NSA_HEADS, 3)
            merged = (g3[..., 0:1] * put(o_cmp, oc_s).reshape(Ttot, NSA_HEADS, hd)
                      + g3[..., 1:2] * put(o_slc, os_s).reshape(Ttot, NSA_HEADS, hd)
                      + g3[..., 2:3] * put(o_win, ow_s).reshape(Ttot, NSA_HEADS, hd)).reshape(Ttot, hq)
            y = _mm(merged, w_nsa_out, layer=li)
        xt = _post_norm(xt, y, ln_gain[l, 0], ln_bias[l, 0])
        yt = _moe(xt, w_router, b_router, w_exp_gate, w_exp_up, w_exp_down, l)
        xt = _post_norm(xt, yt, ln_gain[l, 1], ln_bias[l, 1])

    xp = xt[:P].reshape(B, S, D)
    xs = xt[P:].reshape(DB, TP, D)[:, :T]
    win_rows_p = win_rows[:P].reshape(B, S, 2, G, hd)
    if S >= wb:
        win_p = win_rows_p[:, S - wb:]
    else:
        win_p = jnp.pad(win_rows_p, ((0, 0), (wb - S, 0), (0, 0), (0, 0), (0, 0)))
    win_s = win_keys_s[:, T:]
    return (xp, xs, rows_p, rows_s, win_p, win_s, jnp.stack(ret_p), jnp.stack(ret_s))
```

```python
import functools

import jax
import jax.numpy as jnp
from jax import lax
from jax.experimental import pallas as pl
from jax.experimental.pallas import tpu as pltpu

f32 = jnp.float32
bf16 = jnp.bfloat16

D_MODEL = 2048
DEPTH = 4
PAGE_SIZE = 128
N_A_LAYERS = DEPTH // 2
RET_HEADS = 8
RET_DK = D_MODEL // RET_HEADS
RET_DV = 2 * RET_DK
RET_CHUNK = 128
RET_THETA = 10000.0
NSA_HEADS = 16
NSA_HD = D_MODEL // NSA_HEADS
NSA_KV = 4
NSA_HPG = NSA_HEADS // NSA_KV
CMP_LEN = 32
CMP_STRIDE = 16
SLC_LEN = 64
N_SEL = 16
WINDOW = 512
QBLK = 128
ROPE_THETA = 500000.0
ROT_DIM = NSA_HD // 4
N_KV_PROJ = 6
N_EXPERTS = 16
N_GROUPS = 4
EXPERTS_PER_GROUP = N_EXPERTS // N_GROUPS
TOPK_GROUPS = 1
TOP_K = 2
D_EXPERT = 1408
DN_ALPHA = (2.0 * DEPTH) ** 0.25
LN_EPS = 1e-5

TP = 16
MOE_TM = 512
MOE_TF = 128
VMEM_LIMIT = 56 * 1024 * 1024
NEG = -1e30


def _cparams(sem):
    return pltpu.CompilerParams(dimension_semantics=sem, vmem_limit_bytes=VMEM_LIMIT)


def _bdot(a, b):
    return jnp.dot(a.astype(bf16), b.astype(bf16), preferred_element_type=f32)


def _bdot_nt(a, b):
    return lax.dot_general(a.astype(bf16), b.astype(bf16), (((1,), (1,)), ((), ())), preferred_element_type=f32)


def _mm_kernel(x_ref, w_ref, o_ref):
    o_ref[...] = _bdot(x_ref[...], w_ref[...])


def _mm3_kernel(x_ref, w_ref, o_ref):
    x = x_ref[...]
    w = w_ref[...]
    xh = x.astype(bf16)
    wh = w.astype(bf16)
    xl = (x - xh.astype(f32)).astype(bf16)
    wl = (w - wh.astype(f32)).astype(bf16)
    acc = jnp.dot(xh, wh, preferred_element_type=f32)
    acc += jnp.dot(xh, wl, preferred_element_type=f32)
    acc += jnp.dot(xl, wh, preferred_element_type=f32)
    o_ref[...] = acc


def _row_tile(m, pref=640):
    for t in (pref, 512, 256, 128):
        if m % t == 0:
            return t
    return min(512, m)


def _mm(x, w, layer=None, tn=1024, body=_mm_kernel):
    M, K = x.shape
    N = w.shape[-1]
    tn = min(tn, N)
    tm = _row_tile(M)
    if layer is None:
        w_spec = pl.BlockSpec((K, tn), lambda j, i: (0, j))
    else:
        w_spec = pl.BlockSpec((None, K, tn), lambda j, i: (layer, 0, j))
    return pl.pallas_call(
        body,
        grid=(pl.cdiv(N, tn), pl.cdiv(M, tm)),
        in_specs=[pl.BlockSpec((tm, K), lambda j, i: (i, 0)), w_spec],
        out_specs=pl.BlockSpec((tm, tn), lambda j, i: (i, j)),
        out_shape=jax.ShapeDtypeStruct((M, N), f32),
        compiler_params=_cparams(("parallel", "parallel")),
        name="mm",
    )(x, w)


def _layer_norm(h, g, b):
    mu = jnp.mean(h, -1, keepdims=True)
    d = h - mu
    var = jnp.mean(d * d, -1, keepdims=True)
    return d * lax.rsqrt(var + LN_EPS) * g + b


def _pn_kernel(x_ref, y_ref, g_ref, b_ref, o_ref, ob_ref):
    o = _layer_norm(DN_ALPHA * x_ref[...] + y_ref[...], g_ref[...], b_ref[...])
    o_ref[...] = o
    ob_ref[...] = o.astype(bf16)


def _pn_combine_kernel(x_ref, ya_ref, yb_ref, w_ref, g_ref, b_ref, o_ref):
    w = w_ref[...]
    y = ya_ref[...] * w[:, 0:1] + yb_ref[...] * w[:, 1:2]
    o_ref[...] = _layer_norm(DN_ALPHA * x_ref[...] + y, g_ref[...], b_ref[...])


def _post_norm(x, y, g, b):
    M, D = x.shape
    tm = _row_tile(M, 128)
    row = pl.BlockSpec((tm, D), lambda i: (i, 0))
    vec = pl.BlockSpec((1, D), lambda i: (0, 0))
    return pl.pallas_call(
        _pn_kernel,
        grid=(pl.cdiv(M, tm),),
        in_specs=[row, row, vec, vec],
        out_specs=[row, row],
        out_shape=[jax.ShapeDtypeStruct((M, D), f32), jax.ShapeDtypeStruct((M, D), bf16)],
        compiler_params=_cparams(("parallel",)),
        name="post_norm",
    )(x, y, g.reshape(1, D), b.reshape(1, D))


def _post_norm_combine(x, ya, yb, w, g, b):
    M, D = x.shape
    tm = _row_tile(M, 128)
    row = pl.BlockSpec((tm, D), lambda i: (i, 0))
    vec = pl.BlockSpec((1, D), lambda i: (0, 0))
    return pl.pallas_call(
        _pn_combine_kernel,
        grid=(pl.cdiv(M, tm),),
        in_specs=[row, row, row, pl.BlockSpec((tm, TOP_K), lambda i: (i, 0)), vec, vec],
        out_specs=row,
        out_shape=jax.ShapeDtypeStruct((M, D), f32),
        compiler_params=_cparams(("parallel",)),
        name="post_norm_combine",
    )(x, ya, yb, w, g.reshape(1, D), b.reshape(1, D))


def _ret_kernel(q_ref, k_ref, v_ref, g_ref, cos_ref, sin_ref, intra_ref, qd_ref, kd_ref, cd_ref, s0_ref,
                o_ref, sfin_ref, s_scr, *, nc):
    c = pl.program_id(2)

    @pl.when(c == 0)
    def _():
        s_scr[...] = s0_ref[...]

    half = RET_DK // 2
    cos = cos_ref[...]
    sin = sin_ref[...]

    def rot(t):
        t1, t2 = t[:, :half], t[:, half:]
        return jnp.concatenate([t1 * cos - t2 * sin, t2 * cos + t1 * sin], axis=-1)

    q = rot(q_ref[...])
    k = rot(k_ref[...]) * (RET_DK ** -0.5)
    v = v_ref[...].astype(bf16)
    s = s_scr[...]
    att = _bdot_nt(q, k) * intra_ref[...]
    o = _bdot(att, v) + _bdot(q * qd_ref[...], s)
    kd = (k * kd_ref[...]).T
    s_new = s * cd_ref[...] + _bdot(kd, v)
    s_scr[...] = s_new
    mu = jnp.mean(o, -1, keepdims=True)
    d = o - mu
    var = jnp.mean(d * d, -1, keepdims=True)
    gate = g_ref[...]
    o_ref[...] = gate * jax.nn.sigmoid(gate) * (d * lax.rsqrt(var + LN_EPS))

    @pl.when(c == nc - 1)
    def _():
        sfin_ref[...] = s_new


def _ret_decay(chunk, valid):
    log_g = jnp.log(1.0 - 2.0 ** (-5.0 - jnp.arange(RET_HEADS, dtype=f32)))
    i = jnp.arange(chunk, dtype=f32)
    real = i < valid
    diff = i[:, None] - i[None, :]
    intra = jnp.where((diff >= 0) & real[None, :], jnp.exp(log_g[:, None, None] * jnp.maximum(diff, 0.0)), 0.0)
    q_dec = jnp.exp(log_g[:, None] * (i + 1.0))[:, :, None]
    k_dec = jnp.where(real, jnp.exp(log_g[:, None] * jnp.maximum(valid - 1.0 - i, 0.0)), 0.0)[:, :, None]
    c_dec = jnp.exp(log_g * valid)[:, None, None]
    return intra, q_dec, k_dec, c_dec


def _ret_tables(pos):
    inv = 1.0 / (RET_THETA ** jnp.linspace(0.0, 1.0, RET_DK // 2, dtype=f32))
    ang = pos.astype(f32)[:, None] * inv[None, :]
    return jnp.cos(ang), jnp.sin(ang)


def _retention(proj, pos, s0, nb, nc, chunk, valid, row0, o_prev):
    Ttot = proj.shape[0]
    hk, hv = RET_HEADS * RET_DK, RET_HEADS * RET_DV
    rb0 = row0 // chunk
    cos, sin = _ret_tables(pos)
    intra, q_dec, k_dec, c_dec = _ret_decay(chunk, valid)
    H = RET_HEADS

    def rowblk(b, h, c):
        return rb0 + b * nc + c

    in_specs = [
        pl.BlockSpec((chunk, RET_DK), lambda b, h, c: (rowblk(b, h, c), h)),
        pl.BlockSpec((chunk, RET_DK), lambda b, h, c: (rowblk(b, h, c), H + h)),
        pl.BlockSpec((chunk, RET_DV), lambda b, h, c: (rowblk(b, h, c), (2 * hk) // RET_DV + h)),
        pl.BlockSpec((chunk, RET_DV), lambda b, h, c: (rowblk(b, h, c), (2 * hk + hv) // RET_DV + h)),
        pl.BlockSpec((chunk, RET_DK // 2), lambda b, h, c: (c, 0)),
        pl.BlockSpec((chunk, RET_DK // 2), lambda b, h, c: (c, 0)),
        pl.BlockSpec((None, chunk, chunk), lambda b, h, c: (h, 0, 0)),
        pl.BlockSpec((None, chunk, 1), lambda b, h, c: (h, 0, 0)),
        pl.BlockSpec((None, chunk, 1), lambda b, h, c: (h, 0, 0)),
        pl.BlockSpec((None, 1, 1), lambda b, h, c: (h, 0, 0)),
        pl.BlockSpec((None, None, RET_DK, RET_DV), lambda b, h, c: (b, h, 0, 0)),
    ]
    args = [proj, proj, proj, proj, cos, sin, intra, q_dec, k_dec, c_dec, s0]
    aliases = {}
    if o_prev is not None:
        in_specs.append(pl.BlockSpec(memory_space=pl.ANY))
        args.append(o_prev)
        aliases = {len(args) - 1: 0}

    def body(*refs):
        if o_prev is not None:
            refs = refs[:11] + refs[12:]
        _ret_kernel(*refs, nc=nc)

    return pl.pallas_call(
        body,
        grid=(nb, H, nc),
        in_specs=in_specs,
        out_specs=[
            pl.BlockSpec((chunk, RET_DV), lambda b, h, c: (rowblk(b, h, c), h)),
            pl.BlockSpec((None, None, RET_DK, RET_DV), lambda b, h, c: (b, h, 0, 0)),
        ],
        out_shape=[jax.ShapeDtypeStruct((Ttot, hv), f32), jax.ShapeDtypeStruct((nb, H, RET_DK, RET_DV), f32)],
        scratch_shapes=[pltpu.VMEM((RET_DK, RET_DV), f32)],
        input_output_aliases=aliases,
        compiler_params=_cparams(("parallel", "parallel", "arbitrary")),
        name="retention",
    )(*args)


def _moe_kernel(be_ref, nu_ref, x_ref, wga_ref, wgb_ref, wua_ref, wub_ref, wda_ref, wdb_ref, o_ref, *, nf):
    i = pl.program_id(0)
    j = pl.program_id(1)

    @pl.when(i < nu_ref[0])
    def _():
        xb = x_ref[...]
        wg = jnp.concatenate([wga_ref[...].astype(bf16), wgb_ref[...].astype(bf16)], axis=1)
        wu = jnp.concatenate([wua_ref[...].astype(bf16), wub_ref[...].astype(bf16)], axis=1)
        wd = jnp.concatenate([wda_ref[...].astype(bf16), wdb_ref[...].astype(bf16)], axis=0)
        g = jnp.dot(xb, wg, preferred_element_type=f32)
        u = jnp.dot(xb, wu, preferred_element_type=f32)
        h = g * jax.nn.sigmoid(g) * u
        live = jnp.where(2 * j + 1 < nf, 2 * MOE_TF, MOE_TF)
        h = jnp.where(lax.broadcasted_iota(jnp.int32, h.shape, 1) < live, h, 0.0)
        y = jnp.dot(h.astype(bf16), wd, preferred_element_type=f32)

        @pl.when(j == 0)
        def _():
            o_ref[...] = y

        @pl.when(j > 0)
        def _():
            o_ref[...] += y


def _moe_experts(xs, blk_e, n_used, w_g, w_u, w_d, layer):
    R, D = xs.shape
    n_blk = R // MOE_TM
    nf = D_EXPERT // MOE_TF
    nsteps = -(-nf // 2)

    def ii(i, nu):
        return jnp.minimum(i, nu[0] - 1)

    def jj(i, j, nu, odd):
        return jnp.minimum(2 * jnp.where(i < nu[0], j, nsteps - 1) + odd, nf - 1)

    def w_in_spec(odd):
        return pl.BlockSpec((None, None, D, MOE_TF),
                            lambda i, j, be, nu: (layer, be[ii(i, nu)], 0, jj(i, j, nu, odd)))

    def w_out_spec(odd):
        return pl.BlockSpec((None, None, MOE_TF, D),
                            lambda i, j, be, nu: (layer, be[ii(i, nu)], jj(i, j, nu, odd), 0))

    grid_spec = pltpu.PrefetchScalarGridSpec(
        num_scalar_prefetch=2,
        grid=(n_blk, nsteps),
        in_specs=[
            pl.BlockSpec((MOE_TM, D), lambda i, j, be, nu: (ii(i, nu), 0)),
            w_in_spec(0), w_in_spec(1), w_in_spec(0), w_in_spec(1), w_out_spec(0), w_out_spec(1),
        ],
        out_specs=pl.BlockSpec((MOE_TM, D), lambda i, j, be, nu: (ii(i, nu), 0)),
    )
    return pl.pallas_call(
        functools.partial(_moe_kernel, nf=nf),
        grid_spec=grid_spec,
        out_shape=jax.ShapeDtypeStruct((R, D), f32),
        compiler_params=_cparams(("arbitrary", "arbitrary")),
        name="moe_experts",
    )(blk_e, n_used, xs, w_g, w_g, w_u, w_u, w_d, w_d)


def _topk_small(x, k):
    lane = jnp.arange(x.shape[-1])
    vals, idxs = [], []
    for _ in range(k):
        i = jnp.argmax(x, axis=-1)
        vals.append(jnp.max(x, axis=-1))
        idxs.append(i)
        x = jnp.where(lane == i[..., None], -jnp.inf, x)
    return jnp.stack(vals, -1), jnp.stack(idxs, -1)


def _moe(x, xb, w_router, b_router, w_g, w_u, w_d, layer):
    T, D = x.shape
    logits = _mm(x, w_router, body=_mm3_kernel)
    aff = jax.nn.sigmoid(logits)
    biased = aff + b_router.astype(f32)
    grp = biased.reshape(T, N_GROUPS, EXPERTS_PER_GROUP)
    grp_score = _topk_small(grp, 2)[0].sum(-1)
    _, top_g = _topk_small(grp_score, TOPK_GROUPS)
    g_mask = jnp.any(top_g[:, :, None] == jnp.arange(N_GROUPS)[None, None, :], axis=1)
    e_mask = jnp.repeat(g_mask, EXPERTS_PER_GROUP, axis=1)
    _, idx = _topk_small(jnp.where(e_mask, biased, -jnp.inf), TOP_K)
    w = jnp.take_along_axis(aff, idx, axis=1)
    w = w / jnp.sum(w, -1, keepdims=True)
    A = T * TOP_K
    e_flat = idx.reshape(A).astype(jnp.int32)
    order = jnp.argsort(e_flat)
    e_s = e_flat[order]
    counts = jnp.bincount(e_flat, length=N_EXPERTS)
    start = jnp.cumsum(counts) - counts
    padded = (counts + MOE_TM - 1) // MOE_TM * MOE_TM
    pend = jnp.cumsum(padded)
    pstart = pend - padded
    dest = (pstart[e_s] + (jnp.arange(A) - start[e_s])).astype(jnp.int32)
    n_blk = -(-A // MOE_TM) + N_EXPERTS
    rows_src = (jnp.arange(n_blk * MOE_TM, dtype=jnp.int32) % T).at[dest].set((order // TOP_K).astype(jnp.int32))
    xs = xb[rows_src]
    blk_e = jnp.minimum(jnp.searchsorted(pend, jnp.arange(n_blk) * MOE_TM, side='right'), N_EXPERTS - 1)
    n_used = (pend[-1] // MOE_TM).astype(jnp.int32).reshape(1)
    out = _moe_experts(xs, blk_e.astype(jnp.int32), n_used, w_g, w_u, w_d, layer)
    pos = jnp.zeros((A,), jnp.int32).at[order].set(dest).reshape(T, TOP_K)
    return out[pos[:, 0]], out[pos[:, 1]], w


def _nsa_inv_freq():
    return ROPE_THETA ** (-jnp.arange(0, ROT_DIM, 2, dtype=f32) / ROT_DIM)


def _rope_heads(x, pos):
    T = x.shape[0]
    half = ROT_DIM // 2
    ang = pos.astype(f32)[:, None] * _nsa_inv_freq()[None, :]
    cos = jnp.cos(ang)[:, None, :]
    sin = jnp.sin(ang)[:, None, :]
    xh = x.reshape(T, -1, NSA_HD)
    x1, x2 = xh[..., :half], xh[..., half:ROT_DIM]
    out = jnp.concatenate([x1 * cos - x2 * sin, x2 * cos + x1 * sin, xh[..., ROT_DIM:]], axis=-1)
    return out.reshape(x.shape)


def _masked_softmax(s, mask):
    s = jnp.where(mask, s, NEG)
    m = jnp.max(s, axis=-1, keepdims=True)
    m = jnp.where(m <= 0.5 * NEG, 0.0, m)
    e = jnp.where(mask, jnp.exp(s - m), 0.0)
    den = jnp.sum(e, axis=-1, keepdims=True)
    return e / jnp.where(den > 0, den, 1.0)


def _compress(chunks, w1, w2, pe):
    n, nchunk, kk = chunks.shape
    wab = jnp.concatenate([w1[:kk], w1[kk:]], axis=1)
    y = _mm(chunks.reshape(n * nchunk, kk), wab).reshape(n, nchunk, 2 * NSA_HD)
    bias = _mm(jnp.broadcast_to(pe.reshape(1, CMP_LEN * NSA_HD), (8, CMP_LEN * NSA_HD)), w1)[:1]
    pre = y[:, :-1, :NSA_HD] + y[:, 1:, NSA_HD:] + bias
    nc = nchunk - 1
    return _mm(jax.nn.silu(pre).reshape(n * nc, NSA_HD), w2).reshape(n, nc, NSA_HD)


CMP_P = 8


def _cmp_pages_kernel(pt_ref, *refs):
    P = CMP_P
    page_refs, w_ref, o_ref, x_scr = refs[:2 * P], refs[2 * P], refs[2 * P + 1], refs[2 * P + 2]
    cpp = PAGE_SIZE // CMP_STRIDE
    for typ in range(2):
        for r in range(CMP_STRIDE):
            for g in range(NSA_KV):
                for p in range(P):
                    row = (g * P + p) * cpp
                    x_scr[row:row + cpp, r * NSA_HD:(r + 1) * NSA_HD] = (
                        page_refs[typ * P + p][pl.ds(r, cpp, stride=CMP_STRIDE), g, :])
        y = _bdot(x_scr[...], w_ref[typ])
        for g in range(NSA_KV):
            o_ref[typ, g] = y[g * P * cpp:(g + 1) * P * cpp]


def _cmp_pages(cache_kv, page_table, wab):
    DB, n_pages = page_table.shape
    P = CMP_P
    G, d = NSA_KV, NSA_HD
    cpp = PAGE_SIZE // CMP_STRIDE
    kk = CMP_STRIDE * d

    def page_map(p, typ):
        return lambda b, j, pt: (pt[b * n_pages + j * P + p], 0, typ, 0, 0)

    in_specs = [pl.BlockSpec((None, PAGE_SIZE, None, G, d), page_map(p, typ)) for typ in range(2) for p in range(P)]
    in_specs.append(pl.BlockSpec((2, kk, 2 * d), lambda b, j, pt: (0, 0, 0)))
    return pl.pallas_call(
        _cmp_pages_kernel,
        grid_spec=pltpu.PrefetchScalarGridSpec(
            num_scalar_prefetch=1,
            grid=(DB, n_pages // P),
            in_specs=in_specs,
            out_specs=pl.BlockSpec((None, 2, G, P * cpp, 2 * d), lambda b, j, pt: (b, 0, 0, j, 0)),
            scratch_shapes=[pltpu.VMEM((G * P * cpp, kk), f32)],
        ),
        out_shape=jax.ShapeDtypeStruct((DB, 2, G, n_pages * cpp, 2 * d), f32),
        compiler_params=_cparams(("arbitrary", "arbitrary")),
        name="cmp_pages",
    )(page_table.reshape(-1).astype(jnp.int32), *([cache_kv] * (2 * P)), wab)


def _compress_cache(cache_kv, page_table, w_k1, w_k2, pe_k, w_v1, w_v2, pe_v):
    DB = page_table.shape[0]
    G, d = NSA_KV, NSA_HD
    kk = CMP_STRIDE * d
    wab = jnp.stack([jnp.concatenate([w[:kk], w[kk:]], axis=1) for w in (w_k1, w_v1)])
    y = _cmp_pages(cache_kv, page_table, wab)
    out = []
    for typ, (w1, w2, pe) in enumerate(((w_k1, w_k2, pe_k), (w_v1, w_v2, pe_v))):
        bias = _mm(jnp.broadcast_to(pe.reshape(1, CMP_LEN * d), (8, CMP_LEN * d)), w1)[:1]
        pre = y[:, typ, :, :-1, :d] + y[:, typ, :, 1:, d:] + bias
        nc = pre.shape[2]
        c = _mm(jax.nn.silu(pre).reshape(DB * G * nc, d), w2)
        out.append(c.reshape(DB, G, nc, d).transpose(0, 2, 1, 3))
    return out


def _cmp_kernel(q_ref, kc_ref, vc_ref, ov_ref, o_ref, sc_ref):
    qi = pl.program_id(2)
    ncp = kc_ref.shape[0]
    qpos = qi * QBLK + lax.broadcasted_iota(jnp.int32, (QBLK, ncp), 0)
    cend = lax.broadcasted_iota(jnp.int32, (QBLK, ncp), 1) * CMP_STRIDE + (CMP_LEN - 1)
    mask = cend <= qpos
    kc = kc_ref[...].astype(bf16)
    vc = vc_ref[...].astype(bf16)
    ov = ov_ref[...]
    score = jnp.zeros((QBLK, ov.shape[1]), f32)
    for h in range(NSA_HPG):
        q = q_ref[:, h * NSA_HD:(h + 1) * NSA_HD]
        s = _bdot_nt(q, kc) * (NSA_HD ** -0.5)
        p = _masked_softmax(s, mask).astype(bf16)
        o_ref[:, h * NSA_HD:(h + 1) * NSA_HD] = jnp.dot(p, vc, preferred_element_type=f32)
        score += jnp.dot(p, ov, preferred_element_type=f32)
    sc_ref[...] = score


def _cmp_prompt(q, kc, vc, overlap, B, S):
    Ttot = q.shape[0]
    nq = S // QBLK
    ncp = kc.shape[1]
    ns = overlap.shape[1]
    G = NSA_KV
    gw = NSA_HPG * NSA_HD
    return pl.pallas_call(
        _cmp_kernel,
        grid=(B, G, nq),
        in_specs=[
            pl.BlockSpec((QBLK, gw), lambda b, g, i: (b * nq + i, g)),
            pl.BlockSpec((None, ncp, NSA_HD), lambda b, g, i: (b * G + g, 0, 0)),
            pl.BlockSpec((None, ncp, NSA_HD), lambda b, g, i: (b * G + g, 0, 0)),
            pl.BlockSpec((ncp, ns), lambda b, g, i: (0, 0)),
        ],
        out_specs=[
            pl.BlockSpec((QBLK, gw), lambda b, g, i: (b * nq + i, g)),
            pl.BlockSpec((None, QBLK, ns), lambda b, g, i: (b * G + g, i, 0)),
        ],
        out_shape=[jax.ShapeDtypeStruct((Ttot, G * gw), f32), jax.ShapeDtypeStruct((B * G, S, ns), f32)],
        compiler_params=_cparams(("parallel", "parallel", "parallel")),
        name="cmp_prompt",
    )(q, kc, vc, overlap)


SEL_TK = 512


LOG2E = 1.4426950408889634
LANES = 128


def _fold_lanes(x, op):
    parts = [x[:, j * LANES:(j + 1) * LANES] for j in range(x.shape[1] // LANES)]
    return functools.reduce(op, parts)


def _slc_win_kernel(q_ref, sel_ref, ks_ref, vs_ref, kw_ref, vw_ref, os_ref, ow_ref, s_scr, m_scr, l_scr, acc_scr):
    qi = pl.program_id(2)
    c2 = (NSA_HD ** -0.5) * LOG2E
    qs = [(q_ref[:, h * NSA_HD:(h + 1) * NSA_HD] * c2).astype(bf16) for h in range(NSA_HPG)]
    sel = sel_ref[...].astype(bf16)
    ns = sel.shape[1]
    bpt = SEL_TK // SLC_LEN
    nl = SEL_TK // LANES

    m_scr[...] = jnp.full(m_scr.shape, NEG, f32)
    l_scr[...] = jnp.zeros(l_scr.shape, f32)
    acc_scr[...] = jnp.zeros(acc_scr.shape, f32)
    qpos = qi * QBLK + lax.broadcasted_iota(jnp.int32, (QBLK, SEL_TK), 0)
    n_tiles = (qi * QBLK + QBLK - 1) // SEL_TK + 1

    def scores(t, carry):
        k0 = pl.multiple_of(t * SEL_TK, SEL_TK)
        k = ks_ref[pl.ds(k0, SEL_TK), :].astype(bf16)
        blk = lax.broadcasted_iota(jnp.int32, (ns, SEL_TK), 1) // SLC_LEN + t * bpt
        expand = jnp.where(blk == lax.broadcasted_iota(jnp.int32, (ns, SEL_TK), 0), 1.0, 0.0).astype(bf16)
        chosen = jnp.dot(sel, expand, preferred_element_type=f32) > 0.5
        kpos = k0 + lax.broadcasted_iota(jnp.int32, (QBLK, SEL_TK), 1)
        bias = jnp.where(chosen & (kpos <= qpos), 0.0, NEG)
        for h in range(NSA_HPG):
            s = _bdot_nt(qs[h], k) + bias
            s_scr[h, t] = s
            m_scr[h] = jnp.maximum(m_scr[h], _fold_lanes(s, jnp.maximum))
        return carry

    lax.fori_loop(0, n_tiles, scores, 0)
    for h in range(NSA_HPG):
        m_scr[h] = jnp.broadcast_to(jnp.max(m_scr[h], axis=-1, keepdims=True), (QBLK, LANES))

    def accumulate(t, carry):
        k0 = pl.multiple_of(t * SEL_TK, SEL_TK)
        v = vs_ref[pl.ds(k0, SEL_TK), :].astype(bf16)
        for h in range(NSA_HPG):
            s = s_scr[h, t]
            m = m_scr[h]
            p = jnp.concatenate([jnp.exp2(s[:, j * LANES:(j + 1) * LANES] - m) for j in range(nl)], axis=-1)
            l_scr[h] += _fold_lanes(p, jnp.add)
            acc_scr[h] += jnp.dot(p.astype(bf16), v, preferred_element_type=f32)
        return carry

    lax.fori_loop(0, n_tiles, accumulate, 0)
    for h in range(NSA_HPG):
        l = jnp.sum(l_scr[h], axis=-1, keepdims=True)
        os_ref[:, h * NSA_HD:(h + 1) * NSA_HD] = acc_scr[h] / l

    wk = WINDOW + QBLK
    w0 = pl.multiple_of(jnp.maximum(qi * QBLK - WINDOW, 0), QBLK)
    kw = kw_ref[pl.ds(w0, wk), :].astype(bf16)
    vw = vw_ref[pl.ds(w0, wk), :].astype(bf16)
    qp = qi * QBLK + lax.broadcasted_iota(jnp.int32, (QBLK, wk), 0)
    kp = w0 + lax.broadcasted_iota(jnp.int32, (QBLK, wk), 1)
    wbias = jnp.where((kp <= qp) & (kp >= qp - WINDOW), 0.0, NEG)
    for h in range(NSA_HPG):
        s = _bdot_nt(qs[h], kw) + wbias
        p = jnp.exp2(s - jnp.max(s, axis=-1, keepdims=True))
        o = jnp.dot(p.astype(bf16), vw, preferred_element_type=f32)
        ow_ref[:, h * NSA_HD:(h + 1) * NSA_HD] = o / jnp.sum(p, axis=-1, keepdims=True)


def _slc_win_prompt(qr, selmask, rows, win_rows, B, S):
    Ttot = qr.shape[0]
    nq = S // QBLK
    ns = selmask.shape[-1]
    G = NSA_KV
    gw = NSA_HPG * NSA_HD
    qspec = pl.BlockSpec((QBLK, gw), lambda b, g, i: (b * nq + i, g))
    return pl.pallas_call(
        _slc_win_kernel,
        grid=(B, G, nq),
        in_specs=[
            qspec,
            pl.BlockSpec((None, QBLK, ns), lambda b, g, i: (b * G + g, i, 0)),
            pl.BlockSpec((S, NSA_HD), lambda b, g, i: (b, 2 * G + g)),
            pl.BlockSpec((S, NSA_HD), lambda b, g, i: (b, 3 * G + g)),
            pl.BlockSpec((S, NSA_HD), lambda b, g, i: (b, g)),
            pl.BlockSpec((S, NSA_HD), lambda b, g, i: (b, G + g)),
        ],
        out_specs=[qspec, qspec],
        out_shape=[jax.ShapeDtypeStruct((Ttot, G * gw), f32)] * 2,
        scratch_shapes=[
            pltpu.VMEM((NSA_HPG, S // SEL_TK, QBLK, SEL_TK), f32),
            pltpu.VMEM((NSA_HPG, QBLK, LANES), f32),
            pltpu.VMEM((NSA_HPG, QBLK, LANES), f32),
            pltpu.VMEM((NSA_HPG, QBLK, NSA_HD), f32),
        ],
        compiler_params=_cparams(("parallel", "parallel", "arbitrary")),
        name="slc_win_prompt",
    )(qr, selmask, rows, rows, win_rows, win_rows)


def _overlap(nc, ns):
    c0 = jnp.arange(nc) * CMP_STRIDE
    j0 = jnp.arange(ns) * SLC_LEN
    return ((c0[:, None] < j0[None, :] + SLC_LEN) & (c0[:, None] + CMP_LEN > j0[None, :]))


def _select(score, qpos, ns):
    tblk = qpos // SLC_LEN
    j = jnp.arange(ns)
    forced = (j[None, :] == 0) | (j[None, :] == tblk[:, None]) | (j[None, :] == tblk[:, None] - 1)
    score = jnp.where(forced, jnp.inf, jnp.where(j[None, :] > tblk[:, None], -jnp.inf, score))
    _, idx = lax.top_k(score, min(N_SEL, ns))
    return idx


def _cmp_attention_s(q, kc, vc, qpos):
    nc = kc.shape[1]
    cend = jnp.arange(nc) * CMP_STRIDE + (CMP_LEN - 1)
    mask = cend[None, :] <= qpos[:, None]
    s = jnp.einsum('bqghd,bcgd->bghqc', q, kc, preferred_element_type=f32) * (NSA_HD ** -0.5)
    p = _masked_softmax(s, mask)
    o = jnp.einsum('bghqc,bcgd->bqghd', p, vc)
    return o, p


def _attend_s(q, k, v, mask):
    s = jnp.einsum('bqghd,bkgd->bghqk', q, k, preferred_element_type=f32) * (NSA_HD ** -0.5)
    p = _masked_softmax(s, mask[None, None, None, :, :])
    return jnp.einsum('bghqk,bkgd->bqghd', p, v)


HPAD = 16


def _slc_s_kernel(sel_ref, pt_ref, tsel_ref, q_ref, tk_ref, tv_ref, *refs, n_sel, npb, nt, past, T):
    k_refs, v_refs, o_ref = refs[:n_sel], refs[n_sel:2 * n_sel], refs[2 * n_sel]
    b, g, t = pl.program_id(0), pl.program_id(1), pl.program_id(2)
    flat = (b * NSA_KV + g) * T + t
    q = q_ref[...].astype(bf16)
    qpos = past + t
    lane = lax.broadcasted_iota(jnp.int32, (HPAD, SLC_LEN), 1)
    scale = NSA_HD ** -0.5
    parts = []
    for n in range(n_sel):
        bid = sel_ref[flat * n_sel + n]
        lim = jnp.where(bid < npb, qpos - bid * SLC_LEN, -1)
        mask = lane <= lim
        parts.append((jnp.where(mask, _bdot_nt(q, k_refs[n][:, g, :]) * scale, NEG), mask, v_refs[n][:, g, :]))
    for j in range(nt):
        lim = jnp.where(tsel_ref[flat * nt + j] > 0, qpos - (npb + j) * SLC_LEN, -1)
        mask = lane <= lim
        kt = tk_ref[j * SLC_LEN:(j + 1) * SLC_LEN, g, :]
        parts.append((jnp.where(mask, _bdot_nt(q, kt) * scale, NEG), mask, tv_ref[j * SLC_LEN:(j + 1) * SLC_LEN, g, :]))
    m = functools.reduce(jnp.maximum, [jnp.max(s, axis=-1, keepdims=True) for s, _, _ in parts])
    m = jnp.where(m <= 0.5 * NEG, 0.0, m)
    den = jnp.zeros((HPAD, 1), f32)
    acc = jnp.zeros((HPAD, NSA_HD), f32)
    for s, mask, v in parts:
        e = jnp.where(mask, jnp.exp(s - m), 0.0)
        den += jnp.sum(e, axis=-1, keepdims=True)
        acc += _bdot(e, v)
    o_ref[...] = acc / jnp.where(den > 0, den, 1.0)


def _slc_sample(qr, sel, tail, cache_kv, page_table, past):
    DB, T, G, H, d = qr.shape
    n_sel = sel.shape[-1]
    sub = PAGE_SIZE // SLC_LEN
    npb = past // SLC_LEN
    nt = tail.shape[1] // SLC_LEN
    n_pages = page_table.shape[1]
    q8 = jnp.pad(qr.transpose(0, 2, 1, 3, 4), ((0, 0), (0, 0), (0, 0), (0, HPAD - H), (0, 0)))
    q8 = q8.reshape(DB * G * T, HPAD, d)
    tsel = jnp.any(sel[..., None] == npb + jnp.arange(nt), axis=-2).astype(jnp.int32)

    def cache_map(n, typ):
        def f(b, g, t, sel_r, pt_r, ts_r):
            bid = sel_r[((b * G + g) * T + t) * n_sel + n]
            jp = jnp.clip(bid, 0, npb - 1)
            return (pt_r[b * n_pages + jp // sub], jp % sub, typ, 0, 0)
        return f

    in_specs = [
        pl.BlockSpec((None, HPAD, d), lambda b, g, t, *_: ((b * G + g) * T + t, 0, 0)),
        pl.BlockSpec((None, nt * SLC_LEN, None, G, d), lambda b, g, t, *_: (b, 0, 2, 0, 0)),
        pl.BlockSpec((None, nt * SLC_LEN, None, G, d), lambda b, g, t, *_: (b, 0, 3, 0, 0)),
    ]
    in_specs += [pl.BlockSpec((None, SLC_LEN, None, G, d), cache_map(n, 2)) for n in range(n_sel)]
    in_specs += [pl.BlockSpec((None, SLC_LEN, None, G, d), cache_map(n, 3)) for n in range(n_sel)]
    out = pl.pallas_call(
        functools.partial(_slc_s_kernel, n_sel=n_sel, npb=npb, nt=nt, past=past, T=T),
        grid_spec=pltpu.PrefetchScalarGridSpec(
            num_scalar_prefetch=3,
            grid=(DB, G, T),
            in_specs=in_specs,
            out_specs=pl.BlockSpec((None, HPAD, d), lambda b, g, t, *_: ((b * G + g) * T + t, 0, 0)),
        ),
        out_shape=jax.ShapeDtypeStruct((DB * G * T, HPAD, d), f32),
        compiler_params=_cparams(("arbitrary", "arbitrary", "arbitrary")),
        name="slc_sample",
    )(sel.reshape(-1), page_table.reshape(-1).astype(jnp.int32), tsel.reshape(-1), q8, tail, tail,
      *([cache_kv] * (2 * n_sel)))
    return out.reshape(DB, G, T, HPAD, d)[:, :, :, :H].transpose(0, 2, 1, 3, 4)


def _nsa_sample(q, qr, pos, rows, win_keys, kc, vc, cache_kv, page_table):
    DB, T = q.shape[:2]
    past = page_table.shape[1] * PAGE_SIZE
    sub = PAGE_SIZE // SLC_LEN
    npb = past // SLC_LEN
    ns = -(-(past + T) // SLC_LEN)
    nt = ns - npb
    tail = jnp.pad(rows, ((0, 0), (0, nt * SLC_LEN - T), (0, 0), (0, 0), (0, 0)))
    o_cmp, p_cmp = _cmp_attention_s(q, kc, vc, pos)
    nc = p_cmp.shape[-1]
    score = jnp.einsum('bghqc,cj->bgqj', p_cmp, _overlap(nc, ns).astype(f32))
    sel = _select(score, pos, ns).astype(jnp.int32)
    o_slc = _slc_sample(qr, sel, tail, cache_kv, page_table, past)
    wb = win_keys.shape[1] - T
    kpos_w = past - wb + jnp.arange(wb + T)
    wmask = (kpos_w[None, :] <= pos[:, None]) & (kpos_w[None, :] >= pos[:, None] - WINDOW)
    o_win = _attend_s(qr, win_keys[:, :, 0], win_keys[:, :, 1], wmask)
    return o_cmp, o_slc, o_win


def kernel(x_prompt, x_sample, cache_kv, cache_win, state_ret, page_table,
           w_ret_in, w_ret_out, w_kv, w_cmp_k1, w_cmp_k2, pe_cmp_k, w_cmp_v1, w_cmp_v2, pe_cmp_v,
           w_nsa_in, w_nsa_out, w_router, b_router, w_exp_gate, w_exp_up, w_exp_down, ln_gain, ln_bias):
    B, S, D = x_prompt.shape
    DB, T, _ = x_sample.shape
    past = page_table.shape[1] * PAGE_SIZE
    wb = cache_win.shape[1]
    G, HPG, hd = NSA_KV, NSA_HPG, NSA_HD
    P = B * S
    Ttot = P + DB * TP
    pos_p = jnp.arange(S, dtype=jnp.int32)
    pos_sp = past + jnp.arange(TP, dtype=jnp.int32)
    pos_s = pos_sp[:T]
    pos_flat = jnp.concatenate([jnp.tile(pos_p, B), jnp.tile(pos_sp, DB)])
    chunk = min(RET_CHUNK, S)

    xs_pad = jnp.pad(x_sample, ((0, 0), (0, TP - T), (0, 0)))
    xt = jnp.concatenate([x_prompt.reshape(P, D), xs_pad.reshape(DB * TP, D)], axis=0)

    ret_p, ret_s = [], []
    for l in range(DEPTH):
        if l < N_A_LAYERS:
            proj = _mm(xt, w_ret_in, layer=l)
            s0 = jnp.zeros((B, RET_HEADS, RET_DK, RET_DV), f32)
            o, sp = _retention(proj, pos_p[:chunk * (S // chunk)], s0, B, S // chunk, chunk, float(chunk), 0, None)
            o, ss = _retention(proj, pos_sp, state_ret[l].astype(f32), DB, 1, TP, float(T), P, o)
            ret_p.append(sp.astype(state_ret.dtype))
            ret_s.append(ss.astype(state_ret.dtype))
            y = _mm(o, w_ret_out, layer=l, tn=512)
        else:
            if l == N_A_LAYERS:
                kv = _mm(xt, w_kv)
                gd = G * hd
                rows = jnp.concatenate([kv[:, :2 * gd], _rope_heads(kv[:, 2 * gd:3 * gd], pos_flat),
                                        kv[:, 3 * gd:4 * gd]], axis=1)
                win_rows = jnp.concatenate([_rope_heads(kv[:, 4 * gd:5 * gd], pos_flat), kv[:, 5 * gd:]], axis=1)
                rows_p = rows[:P].reshape(B, S, 4, G, hd)
                rows_s = rows[P:].reshape(DB, TP, 4, G, hd)[:, :T]
                win_rows_s = win_rows[P:].reshape(DB, TP, 2, G, hd)[:, :T]
                win_keys_s = jnp.concatenate([cache_win, win_rows_s.astype(cache_win.dtype)], axis=1)

                def chunks_of(r):
                    n, N = r.shape[:2]
                    nch = N // CMP_STRIDE
                    c = r[:, :nch * CMP_STRIDE].reshape(n, nch, CMP_STRIDE, G, hd)
                    return c.transpose(0, 3, 1, 2, 4).reshape(n * G, nch, CMP_STRIDE * hd)

                def comp(r, w1, w2, pe):
                    n = r.shape[0]
                    c = _compress(chunks_of(r), w1, w2, pe)
                    return c.reshape(n, G, c.shape[1], hd).transpose(0, 2, 1, 3)

                kc_p = comp(rows_p[:, :, 0], w_cmp_k1, w_cmp_k2, pe_cmp_k)
                vc_p = comp(rows_p[:, :, 1], w_cmp_v1, w_cmp_v2, pe_cmp_v)
                assert (past + T) // CMP_STRIDE == past // CMP_STRIDE and page_table.shape[1] % CMP_P == 0
                kc_s, vc_s = _compress_cache(cache_kv, page_table, w_cmp_k1, w_cmp_k2, pe_cmp_k,
                                             w_cmp_v1, w_cmp_v2, pe_cmp_v)
                nc_p = kc_p.shape[1]
                ncp = -(-nc_p // 128) * 128
                ns_p = -(-S // SLC_LEN)

                def pad_c(c):
                    c = jnp.pad(c, ((0, 0), (0, ncp - nc_p), (0, 0), (0, 0)))
                    return c.transpose(0, 2, 1, 3).reshape(B * G, ncp, hd)

                kc_pp, vc_pp = pad_c(kc_p), pad_c(vc_p)
                ov_p = _overlap(ncp, ns_p).astype(bf16)
            li = l - N_A_LAYERS
            hq = NSA_HEADS * hd
            w_in = w_nsa_in[li]
            q = _mm(xt, w_in[:, :hq])
            gates = jax.nn.sigmoid(_mm(xt, w_in[:, hq:]))
            qr = _rope_heads(q, pos_flat)
            o_cmp, score = _cmp_prompt(q, kc_pp, vc_pp, ov_p, B, S)
            sel = _select(score.reshape(B, G, S, ns_p), pos_p, ns_p)
            selmask = jnp.sum(jax.nn.one_hot(sel, ns_p, dtype=f32), axis=-2).reshape(B * G, S, ns_p)
            o_slc, o_win = _slc_win_prompt(qr, selmask, rows, win_rows, B, S)
            q_s = q[P:].reshape(DB, TP, G, HPG, hd)[:, :T]
            qr_s = qr[P:].reshape(DB, TP, G, HPG, hd)[:, :T]
            oc_s, os_s, ow_s = _nsa_sample(q_s, qr_s, pos_s, rows_s, win_keys_s, kc_s, vc_s, cache_kv, page_table)

            def put(o_flat, o_s):
                o_s = jnp.pad(o_s.reshape(DB, T, hq), ((0, 0), (0, TP - T), (0, 0))).reshape(DB * TP, hq)
                return lax.dynamic_update_slice(o_flat, o_s, (P, 0))

            g3 = gates.reshape(Ttot, NSA_HEADS, 3)
            merged = (g3[..., 0:1] * put(o_cmp, oc_s).reshape(Ttot, NSA_HEADS, hd)
                      + g3[..., 1:2] * put(o_slc, os_s).reshape(Ttot, NSA_HEADS, hd)
                      + g3[..., 2:3] * put(o_win, ow_s).reshape(Ttot, NSA_HEADS, hd)).reshape(Ttot, hq)
            y = _mm(merged, w_nsa_out, layer=li)
        xt, xt_b = _post_norm(xt, y, ln_gain[l, 0], ln_bias[l, 0])
        ya, yb, wk = _moe(xt, xt_b, w_router, b_router, w_exp_gate, w_exp_up, w_exp_down, l)
        xt = _post_norm_combine(xt, ya, yb, wk, ln_gain[l, 1], ln_bias[l, 1])

    xp = xt[:P].reshape(B, S, D)
    xs = xt[P:].reshape(DB, TP, D)[:, :T]
    win_rows_p = win_rows[:P].reshape(B, S, 2, G, hd)
    if S >= wb:
        win_p = win_rows_p[:, S - wb:]
    else:
        win_p = jnp.pad(win_rows_p, ((0, 0), (wb - S, 0), (0, 0), (0, 0), (0, 0)))
    win_s = win_keys_s[:, T:]
    return (xp, xs, rows_p, rows_s, win_p, win_s, jnp.stack(ret_p), jnp.stack(ret_s))
```

```python
import functools

import jax
import jax.numpy as jnp
from jax import lax
from jax.experimental import pallas as pl
from jax.experimental.pallas import tpu as pltpu

f32 = jnp.float32
bf16 = jnp.bfloat16

D_MODEL = 2048
DEPTH = 4
PAGE_SIZE = 128
N_A_LAYERS = DEPTH // 2
RET_HEADS = 8
RET_DK = D_MODEL // RET_HEADS
RET_DV = 2 * RET_DK
RET_CHUNK = 128
RET_THETA = 10000.0
NSA_HEADS = 16
NSA_HD = D_MODEL // NSA_HEADS
NSA_KV = 4
NSA_HPG = NSA_HEADS // NSA_KV
CMP_LEN = 32
CMP_STRIDE = 16
SLC_LEN = 64
N_SEL = 16
WINDOW = 512
QBLK = 128
ROPE_THETA = 500000.0
ROT_DIM = NSA_HD // 4
N_KV_PROJ = 6
N_EXPERTS = 16
N_GROUPS = 4
EXPERTS_PER_GROUP = N_EXPERTS // N_GROUPS
TOPK_GROUPS = 1
TOP_K = 2
D_EXPERT = 1408
DN_ALPHA = (2.0 * DEPTH) ** 0.25
LN_EPS = 1e-5

TP = 16
MOE_TM = 512
MOE_TF = 128
VMEM_LIMIT = 56 * 1024 * 1024
NEG = -1e30


def _cparams(sem):
    return pltpu.CompilerParams(dimension_semantics=sem, vmem_limit_bytes=VMEM_LIMIT)


def _bdot(a, b):
    return jnp.dot(a.astype(bf16), b.astype(bf16), preferred_element_type=f32)


def _bdot_nt(a, b):
    return lax.dot_general(a.astype(bf16), b.astype(bf16), (((1,), (1,)), ((), ())), preferred_element_type=f32)


def _mm_kernel(x_ref, w_ref, o_ref, wb_scr):
    @pl.when(pl.program_id(1) == 0)
    def _():
        wb_scr[...] = w_ref[...].astype(bf16)

    o_ref[...] = jnp.dot(x_ref[...].astype(bf16), wb_scr[...], preferred_element_type=f32)


def _mm3_kernel(x_ref, w_ref, o_ref, wb_scr):
    x = x_ref[...]
    w = w_ref[...]
    xh = x.astype(bf16)
    wh = w.astype(bf16)
    xl = (x - xh.astype(f32)).astype(bf16)
    wl = (w - wh.astype(f32)).astype(bf16)
    acc = jnp.dot(xh, wh, preferred_element_type=f32)
    acc += jnp.dot(xh, wl, preferred_element_type=f32)
    acc += jnp.dot(xl, wh, preferred_element_type=f32)
    o_ref[...] = acc


def _row_tile(m, pref=640):
    for t in (pref, 512, 256, 128):
        if m % t == 0:
            return t
    return min(512, m)


def _mm(x, w, layer=None, tn=1024, body=_mm_kernel):
    M, K = x.shape
    N = w.shape[-1]
    tn = min(tn, N)
    tm = _row_tile(M)
    if layer is None:
        w_spec = pl.BlockSpec((K, tn), lambda j, i: (0, j))
    else:
        w_spec = pl.BlockSpec((None, K, tn), lambda j, i: (layer, 0, j))
    return pl.pallas_call(
        body,
        grid=(pl.cdiv(N, tn), pl.cdiv(M, tm)),
        in_specs=[pl.BlockSpec((tm, K), lambda j, i: (i, 0)), w_spec],
        out_specs=pl.BlockSpec((tm, tn), lambda j, i: (i, j)),
        out_shape=jax.ShapeDtypeStruct((M, N), f32),
        scratch_shapes=[pltpu.VMEM((K, tn), bf16)],
        compiler_params=_cparams(("parallel", "arbitrary")),
        name="mm",
    )(x, w)


def _layer_norm(h, g, b):
    mu = jnp.mean(h, -1, keepdims=True)
    d = h - mu
    var = jnp.mean(d * d, -1, keepdims=True)
    return d * lax.rsqrt(var + LN_EPS) * g + b


def _pn_kernel(x_ref, y_ref, g_ref, b_ref, o_ref, ob_ref):
    o = _layer_norm(DN_ALPHA * x_ref[...] + y_ref[...], g_ref[...], b_ref[...])
    o_ref[...] = o
    ob_ref[...] = o.astype(bf16)


def _pn_combine_kernel(x_ref, ya_ref, yb_ref, w_ref, g_ref, b_ref, o_ref, ob_ref):
    w = w_ref[...]
    y = ya_ref[...] * w[:, 0:1] + yb_ref[...] * w[:, 1:2]
    o = _layer_norm(DN_ALPHA * x_ref[...] + y, g_ref[...], b_ref[...])
    o_ref[...] = o
    ob_ref[...] = o.astype(bf16)


def _post_norm(x, y, g, b):
    M, D = x.shape
    tm = _row_tile(M, 128)
    row = pl.BlockSpec((tm, D), lambda i: (i, 0))
    vec = pl.BlockSpec((1, D), lambda i: (0, 0))
    return pl.pallas_call(
        _pn_kernel,
        grid=(pl.cdiv(M, tm),),
        in_specs=[row, row, vec, vec],
        out_specs=[row, row],
        out_shape=[jax.ShapeDtypeStruct((M, D), f32), jax.ShapeDtypeStruct((M, D), bf16)],
        compiler_params=_cparams(("parallel",)),
        name="post_norm",
    )(x, y, g.reshape(1, D), b.reshape(1, D))


def _post_norm_combine(x, ya, yb, w, g, b):
    M, D = x.shape
    tm = _row_tile(M, 128)
    row = pl.BlockSpec((tm, D), lambda i: (i, 0))
    vec = pl.BlockSpec((1, D), lambda i: (0, 0))
    return pl.pallas_call(
        _pn_combine_kernel,
        grid=(pl.cdiv(M, tm),),
        in_specs=[row, row, row, pl.BlockSpec((tm, TOP_K), lambda i: (i, 0)), vec, vec],
        out_specs=[row, row],
        out_shape=[jax.ShapeDtypeStruct((M, D), f32), jax.ShapeDtypeStruct((M, D), bf16)],
        compiler_params=_cparams(("parallel",)),
        name="post_norm_combine",
    )(x, ya, yb, w, g.reshape(1, D), b.reshape(1, D))


RET_HB = 2


def _ret_kernel(q_ref, k_ref, v_ref, g_ref, cos_ref, sin_ref, intra_ref, qd_ref, kd_ref, cd_ref, s0_ref,
                o_ref, sfin_ref, s_scr, *, nc):
    c = pl.program_id(2)

    @pl.when(c == 0)
    def _():
        s_scr[...] = s0_ref[...]

    half = RET_DK // 2
    cos = cos_ref[...]
    sin = sin_ref[...]

    def rot(t):
        t1, t2 = t[:, :half], t[:, half:]
        return jnp.concatenate([t1 * cos - t2 * sin, t2 * cos + t1 * sin], axis=-1)

    for hh in range(RET_HB):
        kcol = slice(hh * RET_DK, (hh + 1) * RET_DK)
        vcol = slice(hh * RET_DV, (hh + 1) * RET_DV)
        q = rot(q_ref[:, kcol])
        k = rot(k_ref[:, kcol]) * (RET_DK ** -0.5)
        v = v_ref[:, vcol].astype(bf16)
        s = s_scr[hh]
        att = _bdot_nt(q, k) * intra_ref[hh]
        o = _bdot(att, v) + _bdot(q * qd_ref[hh], s)
        kd = (k * kd_ref[hh]).T
        s_scr[hh] = s * cd_ref[hh] + _bdot(kd, v)
        mu = jnp.mean(o, -1, keepdims=True)
        d = o - mu
        var = jnp.mean(d * d, -1, keepdims=True)
        gate = g_ref[:, vcol]
        o_ref[:, vcol] = (gate * jax.nn.sigmoid(gate) * (d * lax.rsqrt(var + LN_EPS))).astype(o_ref.dtype)

    @pl.when(c == nc - 1)
    def _():
        sfin_ref[...] = s_scr[...]


def _ret_decay(chunk, valid):
    log_g = jnp.log(1.0 - 2.0 ** (-5.0 - jnp.arange(RET_HEADS, dtype=f32)))
    i = jnp.arange(chunk, dtype=f32)
    real = i < valid
    diff = i[:, None] - i[None, :]
    intra = jnp.where((diff >= 0) & real[None, :], jnp.exp(log_g[:, None, None] * jnp.maximum(diff, 0.0)), 0.0)
    q_dec = jnp.exp(log_g[:, None] * (i + 1.0))[:, :, None]
    k_dec = jnp.where(real, jnp.exp(log_g[:, None] * jnp.maximum(valid - 1.0 - i, 0.0)), 0.0)[:, :, None]
    c_dec = jnp.exp(log_g * valid)[:, None, None]
    return intra, q_dec, k_dec, c_dec


def _ret_tables(pos):
    inv = 1.0 / (RET_THETA ** jnp.linspace(0.0, 1.0, RET_DK // 2, dtype=f32))
    ang = pos.astype(f32)[:, None] * inv[None, :]
    return jnp.cos(ang), jnp.sin(ang)


def _retention(proj, pos, s0, nb, nc, chunk, valid, row0, o_prev):
    Ttot = proj.shape[0]
    hk, hv = RET_HEADS * RET_DK, RET_HEADS * RET_DV
    rb0 = row0 // chunk
    cos, sin = _ret_tables(pos)
    intra, q_dec, k_dec, c_dec = _ret_decay(chunk, valid)
    H = RET_HEADS

    def rowblk(b, h, c):
        return rb0 + b * nc + c

    HB = RET_HB
    wk, wv = HB * RET_DK, HB * RET_DV
    in_specs = [
        pl.BlockSpec((chunk, wk), lambda b, h, c: (rowblk(b, h, c), h)),
        pl.BlockSpec((chunk, wk), lambda b, h, c: (rowblk(b, h, c), hk // wk + h)),
        pl.BlockSpec((chunk, wv), lambda b, h, c: (rowblk(b, h, c), (2 * hk) // wv + h)),
        pl.BlockSpec((chunk, wv), lambda b, h, c: (rowblk(b, h, c), (2 * hk + hv) // wv + h)),
        pl.BlockSpec((chunk, RET_DK // 2), lambda b, h, c: (c, 0)),
        pl.BlockSpec((chunk, RET_DK // 2), lambda b, h, c: (c, 0)),
        pl.BlockSpec((HB, chunk, chunk), lambda b, h, c: (h, 0, 0)),
        pl.BlockSpec((HB, chunk, 1), lambda b, h, c: (h, 0, 0)),
        pl.BlockSpec((HB, chunk, 1), lambda b, h, c: (h, 0, 0)),
        pl.BlockSpec((HB, 1, 1), lambda b, h, c: (h, 0, 0)),
        pl.BlockSpec((None, HB, RET_DK, RET_DV), lambda b, h, c: (b, h, 0, 0)),
    ]
    args = [proj, proj, proj, proj, cos, sin, intra, q_dec, k_dec, c_dec, s0]
    aliases = {}
    if o_prev is not None:
        in_specs.append(pl.BlockSpec(memory_space=pl.ANY))
        args.append(o_prev)
        aliases = {len(args) - 1: 0}

    def body(*refs):
        if o_prev is not None:
            refs = refs[:11] + refs[12:]
        _ret_kernel(*refs, nc=nc)

    return pl.pallas_call(
        body,
        grid=(nb, H // HB, nc),
        in_specs=in_specs,
        out_specs=[
            pl.BlockSpec((chunk, wv), lambda b, h, c: (rowblk(b, h, c), h)),
            pl.BlockSpec((None, HB, RET_DK, RET_DV), lambda b, h, c: (b, h, 0, 0)),
        ],
        out_shape=[jax.ShapeDtypeStruct((Ttot, hv), bf16), jax.ShapeDtypeStruct((nb, H, RET_DK, RET_DV), f32)],
        scratch_shapes=[pltpu.VMEM((HB, RET_DK, RET_DV), f32)],
        input_output_aliases=aliases,
        compiler_params=_cparams(("parallel", "parallel", "arbitrary")),
        name="retention",
    )(*args)


def _moe_kernel(be_ref, nu_ref, x_ref, wga_ref, wgb_ref, wua_ref, wub_ref, wda_ref, wdb_ref, o_ref, *, nf):
    i = pl.program_id(0)
    j = pl.program_id(1)

    @pl.when((i < nu_ref[0]) & (j == 0))
    def _():
        o_ref[...] = jnp.zeros(o_ref.shape, f32)

    @pl.when(i < nu_ref[0])
    def _():
        xb = x_ref[...]
        wg = jnp.concatenate([wga_ref[...].astype(bf16), wgb_ref[...].astype(bf16)], axis=1)
        wu = jnp.concatenate([wua_ref[...].astype(bf16), wub_ref[...].astype(bf16)], axis=1)
        wd = jnp.concatenate([wda_ref[...].astype(bf16), wdb_ref[...].astype(bf16)], axis=0)
        g = jnp.dot(xb, wg, preferred_element_type=f32)
        u = jnp.dot(xb, wu, preferred_element_type=f32)
        h = g * jax.nn.sigmoid(g) * u
        live = jnp.where(2 * j + 1 < nf, 2 * MOE_TF, MOE_TF)
        h = jnp.where(lax.broadcasted_iota(jnp.int32, h.shape, 1) < live, h, 0.0)
        o_ref[...] += jnp.dot(h.astype(bf16), wd, preferred_element_type=f32)


def _moe_experts(xs, blk_e, n_used, w_g, w_u, w_d, layer):
    R, D = xs.shape
    n_blk = R // MOE_TM
    nf = D_EXPERT // MOE_TF
    nsteps = -(-nf // 2)

    def ii(i, nu):
        return jnp.minimum(i, nu[0] - 1)

    def jj(i, j, nu, odd):
        return jnp.minimum(2 * jnp.where(i < nu[0], j, nsteps - 1) + odd, nf - 1)

    def w_in_spec(odd):
        return pl.BlockSpec((None, None, D, MOE_TF),
                            lambda i, j, be, nu: (layer, be[ii(i, nu)], 0, jj(i, j, nu, odd)))

    def w_out_spec(odd):
        return pl.BlockSpec((None, None, MOE_TF, D),
                            lambda i, j, be, nu: (layer, be[ii(i, nu)], jj(i, j, nu, odd), 0))

    grid_spec = pltpu.PrefetchScalarGridSpec(
        num_scalar_prefetch=2,
        grid=(n_blk, nsteps),
        in_specs=[
            pl.BlockSpec((MOE_TM, D), lambda i, j, be, nu: (ii(i, nu), 0)),
            w_in_spec(0), w_in_spec(1), w_in_spec(0), w_in_spec(1), w_out_spec(0), w_out_spec(1),
        ],
        out_specs=pl.BlockSpec((MOE_TM, D), lambda i, j, be, nu: (ii(i, nu), 0)),
    )
    return pl.pallas_call(
        functools.partial(_moe_kernel, nf=nf),
        grid_spec=grid_spec,
        out_shape=jax.ShapeDtypeStruct((R, D), f32),
        compiler_params=_cparams(("arbitrary", "arbitrary")),
        name="moe_experts",
    )(blk_e, n_used, xs, w_g, w_g, w_u, w_u, w_d, w_d)


def _topk_small(x, k):
    lane = jnp.arange(x.shape[-1])
    vals, idxs = [], []
    for _ in range(k):
        i = jnp.argmax(x, axis=-1)
        vals.append(jnp.max(x, axis=-1))
        idxs.append(i)
        x = jnp.where(lane == i[..., None], -jnp.inf, x)
    return jnp.stack(vals, -1), jnp.stack(idxs, -1)


def _moe(x, xb, w_router, b_router, w_g, w_u, w_d, layer):
    T, D = x.shape
    logits = _mm(x, w_router, body=_mm3_kernel)
    aff = jax.nn.sigmoid(logits)
    biased = aff + b_router.astype(f32)
    grp = biased.reshape(T, N_GROUPS, EXPERTS_PER_GROUP)
    grp_score = _topk_small(grp, 2)[0].sum(-1)
    _, top_g = _topk_small(grp_score, TOPK_GROUPS)
    g_mask = jnp.any(top_g[:, :, None] == jnp.arange(N_GROUPS)[None, None, :], axis=1)
    e_mask = jnp.repeat(g_mask, EXPERTS_PER_GROUP, axis=1)
    _, idx = _topk_small(jnp.where(e_mask, biased, -jnp.inf), TOP_K)
    w = jnp.take_along_axis(aff, idx, axis=1)
    w = w / jnp.sum(w, -1, keepdims=True)
    A = T * TOP_K
    e_flat = idx.reshape(A).astype(jnp.int32)
    order = jnp.argsort(e_flat)
    e_s = e_flat[order]
    counts = jnp.bincount(e_flat, length=N_EXPERTS)
    start = jnp.cumsum(counts) - counts
    padded = (counts + MOE_TM - 1) // MOE_TM * MOE_TM
    pend = jnp.cumsum(padded)
    pstart = pend - padded
    dest = (pstart[e_s] + (jnp.arange(A) - start[e_s])).astype(jnp.int32)
    n_blk = -(-A // MOE_TM) + N_EXPERTS
    rows_src = (jnp.arange(n_blk * MOE_TM, dtype=jnp.int32) % T).at[dest].set((order // TOP_K).astype(jnp.int32))
    xs = xb[rows_src]
    blk_e = jnp.minimum(jnp.searchsorted(pend, jnp.arange(n_blk) * MOE_TM, side='right'), N_EXPERTS - 1)
    n_used = (pend[-1] // MOE_TM).astype(jnp.int32).reshape(1)
    out = _moe_experts(xs, blk_e.astype(jnp.int32), n_used, w_g, w_u, w_d, layer)
    pos = jnp.zeros((A,), jnp.int32).at[order].set(dest).reshape(T, TOP_K)
    return out[pos[:, 0]], out[pos[:, 1]], w


def _nsa_inv_freq():
    return ROPE_THETA ** (-jnp.arange(0, ROT_DIM, 2, dtype=f32) / ROT_DIM)


def _rope_heads(x, pos):
    T = x.shape[0]
    half = ROT_DIM // 2
    ang = pos.astype(f32)[:, None] * _nsa_inv_freq()[None, :]
    cos = jnp.cos(ang)[:, None, :]
    sin = jnp.sin(ang)[:, None, :]
    xh = x.reshape(T, -1, NSA_HD)
    x1, x2 = xh[..., :half], xh[..., half:ROT_DIM]
    out = jnp.concatenate([x1 * cos - x2 * sin, x2 * cos + x1 * sin, xh[..., ROT_DIM:]], axis=-1)
    return out.reshape(x.shape)


def _masked_softmax(s, mask):
    s = jnp.where(mask, s, NEG)
    m = jnp.max(s, axis=-1, keepdims=True)
    m = jnp.where(m <= 0.5 * NEG, 0.0, m)
    e = jnp.where(mask, jnp.exp(s - m), 0.0)
    den = jnp.sum(e, axis=-1, keepdims=True)
    return e / jnp.where(den > 0, den, 1.0)


def _compress(chunks, w1, w2, pe):
    n, nchunk, kk = chunks.shape
    wab = jnp.concatenate([w1[:kk], w1[kk:]], axis=1)
    y = _mm(chunks.reshape(n * nchunk, kk), wab).reshape(n, nchunk, 2 * NSA_HD)
    bias = _mm(jnp.broadcast_to(pe.reshape(1, CMP_LEN * NSA_HD), (8, CMP_LEN * NSA_HD)), w1)[:1]
    pre = y[:, :-1, :NSA_HD] + y[:, 1:, NSA_HD:] + bias
    nc = nchunk - 1
    return _mm(jax.nn.silu(pre).reshape(n * nc, NSA_HD), w2).reshape(n, nc, NSA_HD)


CMP_P = 8


def _cmp_pages_kernel(pt_ref, *refs):
    P = CMP_P
    page_refs, w_ref, o_ref, x_scr = refs[:2 * P], refs[2 * P], refs[2 * P + 1], refs[2 * P + 2]
    cpp = PAGE_SIZE // CMP_STRIDE
    for typ in range(2):
        for r in range(CMP_STRIDE):
            for g in range(NSA_KV):
                for p in range(P):
                    row = (g * P + p) * cpp
                    x_scr[row:row + cpp, r * NSA_HD:(r + 1) * NSA_HD] = (
                        page_refs[typ * P + p][pl.ds(r, cpp, stride=CMP_STRIDE), g, :])
        y = _bdot(x_scr[...], w_ref[typ])
        for g in range(NSA_KV):
            o_ref[typ, g] = y[g * P * cpp:(g + 1) * P * cpp]


def _cmp_pages(cache_kv, page_table, wab):
    DB, n_pages = page_table.shape
    P = CMP_P
    G, d = NSA_KV, NSA_HD
    cpp = PAGE_SIZE // CMP_STRIDE
    kk = CMP_STRIDE * d

    def page_map(p, typ):
        return lambda b, j, pt: (pt[b * n_pages + j * P + p], 0, typ, 0, 0)

    in_specs = [pl.BlockSpec((None, PAGE_SIZE, None, G, d), page_map(p, typ)) for typ in range(2) for p in range(P)]
    in_specs.append(pl.BlockSpec((2, kk, 2 * d), lambda b, j, pt: (0, 0, 0)))
    return pl.pallas_call(
        _cmp_pages_kernel,
        grid_spec=pltpu.PrefetchScalarGridSpec(
            num_scalar_prefetch=1,
            grid=(DB, n_pages // P),
            in_specs=in_specs,
            out_specs=pl.BlockSpec((None, 2, G, P * cpp, 2 * d), lambda b, j, pt: (b, 0, 0, j, 0)),
            scratch_shapes=[pltpu.VMEM((G * P * cpp, kk), f32)],
        ),
        out_shape=jax.ShapeDtypeStruct((DB, 2, G, n_pages * cpp, 2 * d), f32),
        compiler_params=_cparams(("arbitrary", "arbitrary")),
        name="cmp_pages",
    )(page_table.reshape(-1).astype(jnp.int32), *([cache_kv] * (2 * P)), wab)


def _compress_cache(cache_kv, page_table, w_k1, w_k2, pe_k, w_v1, w_v2, pe_v):
    DB = page_table.shape[0]
    G, d = NSA_KV, NSA_HD
    kk = CMP_STRIDE * d
    wab = jnp.stack([jnp.concatenate([w[:kk], w[kk:]], axis=1) for w in (w_k1, w_v1)])
    y = _cmp_pages(cache_kv, page_table, wab)
    out = []
    for typ, (w1, w2, pe) in enumerate(((w_k1, w_k2, pe_k), (w_v1, w_v2, pe_v))):
        bias = _mm(jnp.broadcast_to(pe.reshape(1, CMP_LEN * d), (8, CMP_LEN * d)), w1)[:1]
        pre = y[:, typ, :, :-1, :d] + y[:, typ, :, 1:, d:] + bias
        nc = pre.shape[2]
        c = _mm(jax.nn.silu(pre).reshape(DB * G * nc, d), w2)
        out.append(c.reshape(DB, G, nc, d).transpose(0, 2, 1, 3))
    return out


def _cmp_kernel(q_ref, kc_ref, vc_ref, ov_ref, o_ref, sc_ref):
    qi = pl.program_id(2)
    ncp = kc_ref.shape[0]
    qpos = qi * QBLK + lax.broadcasted_iota(jnp.int32, (QBLK, ncp), 0)
    cend = lax.broadcasted_iota(jnp.int32, (QBLK, ncp), 1) * CMP_STRIDE + (CMP_LEN - 1)
    mask = cend <= qpos
    kc = kc_ref[...].astype(bf16)
    vc = vc_ref[...].astype(bf16)
    ov = ov_ref[...]
    score = jnp.zeros((QBLK, ov.shape[1]), f32)
    for h in range(NSA_HPG):
        q = q_ref[:, h * NSA_HD:(h + 1) * NSA_HD]
        s = _bdot_nt(q, kc) * (NSA_HD ** -0.5)
        p = _masked_softmax(s, mask).astype(bf16)
        o_ref[:, h * NSA_HD:(h + 1) * NSA_HD] = jnp.dot(p, vc, preferred_element_type=f32)
        score += jnp.dot(p, ov, preferred_element_type=f32)
    sc_ref[...] = score


def _cmp_prompt(q, kc, vc, overlap, B, S):
    Ttot = q.shape[0]
    nq = S // QBLK
    ncp = kc.shape[1]
    ns = overlap.shape[1]
    G = NSA_KV
    gw = NSA_HPG * NSA_HD
    return pl.pallas_call(
        _cmp_kernel,
        grid=(B, G, nq),
        in_specs=[
            pl.BlockSpec((QBLK, gw), lambda b, g, i: (b * nq + i, g)),
            pl.BlockSpec((None, ncp, NSA_HD), lambda b, g, i: (b * G + g, 0, 0)),
            pl.BlockSpec((None, ncp, NSA_HD), lambda b, g, i: (b * G + g, 0, 0)),
            pl.BlockSpec((ncp, ns), lambda b, g, i: (0, 0)),
        ],
        out_specs=[
            pl.BlockSpec((QBLK, gw), lambda b, g, i: (b * nq + i, g)),
            pl.BlockSpec((None, QBLK, ns), lambda b, g, i: (b * G + g, i, 0)),
        ],
        out_shape=[jax.ShapeDtypeStruct((Ttot, G * gw), f32), jax.ShapeDtypeStruct((B * G, S, ns), f32)],
        compiler_params=_cparams(("parallel", "parallel", "parallel")),
        name="cmp_prompt",
    )(q, kc, vc, overlap)


SEL_TK = 512


LOG2E = 1.4426950408889634
LANES = 128


def _fold_lanes(x, op):
    parts = [x[:, j * LANES:(j + 1) * LANES] for j in range(x.shape[1] // LANES)]
    return functools.reduce(op, parts)


def _slc_win_kernel(q_ref, rc_ref, rs1_ref, rs2_ref, sel_ref, ks_ref, vs_ref, kw_ref, vw_ref, oc_ref, gt_ref, o_ref,
                    s_scr, m_scr, l_scr, acc_scr):
    qi = pl.program_id(2)
    H, d = NSA_HPG, NSA_HD
    c2 = (d ** -0.5) * LOG2E
    rc, rs1, rs2 = rc_ref[...], rs1_ref[...], rs2_ref[...]
    half = ROT_DIM // 2

    def roped(h):
        x = q_ref[:, h * d:(h + 1) * d]
        return x * rc + pltpu.roll(x, d - half, 1) * rs1 + pltpu.roll(x, half, 1) * rs2

    q4 = jnp.concatenate([(roped(h) * c2).astype(bf16) for h in range(H)], axis=0)
    sel = sel_ref[...].astype(bf16)
    ns = sel.shape[1]
    bpt = SEL_TK // SLC_LEN
    nl = SEL_TK // LANES

    m_scr[...] = jnp.full(m_scr.shape, NEG, f32)
    l_scr[...] = jnp.zeros(l_scr.shape, f32)
    acc_scr[...] = jnp.zeros(acc_scr.shape, f32)
    qpos = qi * QBLK + lax.broadcasted_iota(jnp.int32, (QBLK, SEL_TK), 0)
    n_tiles = (qi * QBLK + QBLK - 1) // SEL_TK + 1

    def scores(t, carry):
        k0 = pl.multiple_of(t * SEL_TK, SEL_TK)
        k = ks_ref[pl.ds(k0, SEL_TK), :].astype(bf16)
        blk = lax.broadcasted_iota(jnp.int32, (ns, SEL_TK), 1) // SLC_LEN + t * bpt
        expand = jnp.where(blk == lax.broadcasted_iota(jnp.int32, (ns, SEL_TK), 0), 1.0, 0.0).astype(bf16)
        chosen = jnp.dot(sel, expand, preferred_element_type=f32) > 0.5
        kpos = k0 + lax.broadcasted_iota(jnp.int32, (QBLK, SEL_TK), 1)
        bias = jnp.where(chosen & (kpos <= qpos), 0.0, NEG)
        s4 = _bdot_nt(q4, k)
        for h in range(H):
            rows = slice(h * QBLK, (h + 1) * QBLK)
            s = s4[rows] + bias
            s_scr[t, rows] = s
            m_scr[rows] = jnp.maximum(m_scr[rows], _fold_lanes(s, jnp.maximum))
        return carry

    lax.fori_loop(0, n_tiles, scores, 0)
    m_scr[...] = jnp.broadcast_to(jnp.max(m_scr[...], axis=-1, keepdims=True), m_scr.shape)

    def accumulate(t, carry):
        k0 = pl.multiple_of(t * SEL_TK, SEL_TK)
        v = vs_ref[pl.ds(k0, SEL_TK), :].astype(bf16)
        s = s_scr[t]
        m = m_scr[...]
        p = jnp.concatenate([jnp.exp2(s[:, j * LANES:(j + 1) * LANES] - m) for j in range(nl)], axis=-1)
        l_scr[...] += _fold_lanes(p, jnp.add)
        acc_scr[...] += jnp.dot(p.astype(bf16), v, preferred_element_type=f32)
        return carry

    lax.fori_loop(0, n_tiles, accumulate, 0)
    o_slc = acc_scr[...] / jnp.sum(l_scr[...], axis=-1, keepdims=True)

    wk = WINDOW + QBLK
    w0 = pl.multiple_of(jnp.maximum(qi * QBLK - WINDOW, 0), QBLK)
    kw = kw_ref[pl.ds(w0, wk), :].astype(bf16)
    vw = vw_ref[pl.ds(w0, wk), :].astype(bf16)
    qp = qi * QBLK + lax.broadcasted_iota(jnp.int32, (QBLK, wk), 0)
    kp = w0 + lax.broadcasted_iota(jnp.int32, (QBLK, wk), 1)
    wbias = jnp.where((kp <= qp) & (kp >= qp - WINDOW), 0.0, NEG)
    sw = _bdot_nt(q4, kw)
    sw = jnp.concatenate([sw[h * QBLK:(h + 1) * QBLK] + wbias for h in range(H)], axis=0)
    pw = jnp.exp2(sw - jnp.max(sw, axis=-1, keepdims=True))
    o_win = jnp.dot(pw.astype(bf16), vw, preferred_element_type=f32) / jnp.sum(pw, axis=-1, keepdims=True)

    gt = gt_ref[...]
    for h in range(H):
        rows = slice(h * QBLK, (h + 1) * QBLK)
        o = (gt[:, 3 * h:3 * h + 1] * oc_ref[:, h * d:(h + 1) * d]
             + gt[:, 3 * h + 1:3 * h + 2] * o_slc[rows] + gt[:, 3 * h + 2:3 * h + 3] * o_win[rows])
        o_ref[:, h * d:(h + 1) * d] = o.astype(o_ref.dtype)


def _rope_tables(pos):
    half = ROT_DIM // 2
    ang = pos.astype(f32)[:, None] * _nsa_inv_freq()[None, :]
    cos, sin = jnp.cos(ang), jnp.sin(ang)
    n = pos.shape[0]
    c = jnp.concatenate([cos, cos, jnp.ones((n, NSA_HD - ROT_DIM), f32)], axis=1)
    s1 = jnp.concatenate([-sin, jnp.zeros((n, NSA_HD - half), f32)], axis=1)
    s2 = jnp.concatenate([jnp.zeros((n, half), f32), sin, jnp.zeros((n, NSA_HD - ROT_DIM), f32)], axis=1)
    return c, s1, s2


def _slc_win_prompt(q, pos, selmask, rows, win_rows, o_cmp, gates, B, S):
    Ttot = q.shape[0]
    nq = S // QBLK
    ns = selmask.shape[-1]
    G = NSA_KV
    gw = NSA_HPG * NSA_HD
    rc, rs1, rs2 = _rope_tables(pos)
    qspec = pl.BlockSpec((QBLK, gw), lambda b, g, i: (b * nq + i, g))
    tspec = pl.BlockSpec((QBLK, NSA_HD), lambda b, g, i: (i, 0))
    return pl.pallas_call(
        _slc_win_kernel,
        grid=(B, G, nq),
        in_specs=[
            qspec, tspec, tspec, tspec,
            pl.BlockSpec((None, QBLK, ns), lambda b, g, i: (b * G + g, i, 0)),
            pl.BlockSpec((S, NSA_HD), lambda b, g, i: (b, 2 * G + g)),
            pl.BlockSpec((S, NSA_HD), lambda b, g, i: (b, 3 * G + g)),
            pl.BlockSpec((S, NSA_HD), lambda b, g, i: (b, g)),
            pl.BlockSpec((S, NSA_HD), lambda b, g, i: (b, G + g)),
            qspec,
            pl.BlockSpec((None, QBLK, 3 * NSA_HPG), lambda b, g, i: (g, b * nq + i, 0)),
        ],
        out_specs=qspec,
        out_shape=jax.ShapeDtypeStruct((Ttot, G * gw), bf16),
        scratch_shapes=[
            pltpu.VMEM((S // SEL_TK, NSA_HPG * QBLK, SEL_TK), f32),
            pltpu.VMEM((NSA_HPG * QBLK, LANES), f32),
            pltpu.VMEM((NSA_HPG * QBLK, LANES), f32),
            pltpu.VMEM((NSA_HPG * QBLK, NSA_HD), f32),
        ],
        compiler_params=_cparams(("parallel", "parallel", "arbitrary")),
        name="slc_win_prompt",
    )(q, rc, rs1, rs2, selmask, rows, rows, win_rows, win_rows, o_cmp, gates)


def _overlap(nc, ns):
    c0 = jnp.arange(nc) * CMP_STRIDE
    j0 = jnp.arange(ns) * SLC_LEN
    return ((c0[:, None] < j0[None, :] + SLC_LEN) & (c0[:, None] + CMP_LEN > j0[None, :]))


def _select(score, qpos, ns):
    tblk = qpos // SLC_LEN
    j = jnp.arange(ns)
    forced = (j[None, :] == 0) | (j[None, :] == tblk[:, None]) | (j[None, :] == tblk[:, None] - 1)
    score = jnp.where(forced, jnp.inf, jnp.where(j[None, :] > tblk[:, None], -jnp.inf, score))
    _, idx = lax.top_k(score, min(N_SEL, ns))
    return idx


def _cmp_attention_s(q, kc, vc, qpos):
    nc = kc.shape[1]
    cend = jnp.arange(nc) * CMP_STRIDE + (CMP_LEN - 1)
    mask = cend[None, :] <= qpos[:, None]
    s = jnp.einsum('bqghd,bcgd->bghqc', q, kc, preferred_element_type=f32) * (NSA_HD ** -0.5)
    p = _masked_softmax(s, mask)
    o = jnp.einsum('bghqc,bcgd->bqghd', p, vc)
    return o, p


def _attend_s(q, k, v, mask):
    s = jnp.einsum('bqghd,bkgd->bghqk', q, k, preferred_element_type=f32) * (NSA_HD ** -0.5)
    p = _masked_softmax(s, mask[None, None, None, :, :])
    return jnp.einsum('bghqk,bkgd->bqghd', p, v)


HPAD = 16


def _slc_s_kernel(sel_ref, pt_ref, tsel_ref, q_ref, tk_ref, tv_ref, *refs, n_sel, npb, nt, past, T):
    k_refs, v_refs, o_ref = refs[:n_sel], refs[n_sel:2 * n_sel], refs[2 * n_sel]
    b, g, t = pl.program_id(0), pl.program_id(1), pl.program_id(2)
    flat = (b * NSA_KV + g) * T + t
    q = q_ref[...].astype(bf16)
    qpos = past + t
    lane = lax.broadcasted_iota(jnp.int32, (HPAD, SLC_LEN), 1)
    scale = NSA_HD ** -0.5
    parts = []
    for n in range(n_sel):
        bid = sel_ref[flat * n_sel + n]
        lim = jnp.where(bid < npb, qpos - bid * SLC_LEN, -1)
        mask = lane <= lim
        parts.append((jnp.where(mask, _bdot_nt(q, k_refs[n][:, g, :]) * scale, NEG), mask, v_refs[n][:, g, :]))
    for j in range(nt):
        lim = jnp.where(tsel_ref[flat * nt + j] > 0, qpos - (npb + j) * SLC_LEN, -1)
        mask = lane <= lim
        kt = tk_ref[j * SLC_LEN:(j + 1) * SLC_LEN, g, :]
        parts.append((jnp.where(mask, _bdot_nt(q, kt) * scale, NEG), mask, tv_ref[j * SLC_LEN:(j + 1) * SLC_LEN, g, :]))
    m = functools.reduce(jnp.maximum, [jnp.max(s, axis=-1, keepdims=True) for s, _, _ in parts])
    m = jnp.where(m <= 0.5 * NEG, 0.0, m)
    den = jnp.zeros((HPAD, 1), f32)
    acc = jnp.zeros((HPAD, NSA_HD), f32)
    for s, mask, v in parts:
        e = jnp.where(mask, jnp.exp(s - m), 0.0)
        den += jnp.sum(e, axis=-1, keepdims=True)
        acc += _bdot(e, v)
    o_ref[...] = acc / jnp.where(den > 0, den, 1.0)


def _slc_sample(qr, sel, tail, cache_kv, page_table, past):
    DB, T, G, H, d = qr.shape
    n_sel = sel.shape[-1]
    sub = PAGE_SIZE // SLC_LEN
    npb = past // SLC_LEN
    nt = tail.shape[1] // SLC_LEN
    n_pages = page_table.shape[1]
    q8 = jnp.pad(qr.transpose(0, 2, 1, 3, 4), ((0, 0), (0, 0), (0, 0), (0, HPAD - H), (0, 0)))
    q8 = q8.reshape(DB * G * T, HPAD, d)
    tsel = jnp.any(sel[..., None] == npb + jnp.arange(nt), axis=-2).astype(jnp.int32)

    def cache_map(n, typ):
        def f(b, g, t, sel_r, pt_r, ts_r):
            bid = sel_r[((b * G + g) * T + t) * n_sel + n]
            jp = jnp.clip(bid, 0, npb - 1)
            return (pt_r[b * n_pages + jp // sub], jp % sub, typ, 0, 0)
        return f

    in_specs = [
        pl.BlockSpec((None, HPAD, d), lambda b, g, t, *_: ((b * G + g) * T + t, 0, 0)),
        pl.BlockSpec((None, nt * SLC_LEN, None, G, d), lambda b, g, t, *_: (b, 0, 2, 0, 0)),
        pl.BlockSpec((None, nt * SLC_LEN, None, G, d), lambda b, g, t, *_: (b, 0, 3, 0, 0)),
    ]
    in_specs += [pl.BlockSpec((None, SLC_LEN, None, G, d), cache_map(n, 2)) for n in range(n_sel)]
    in_specs += [pl.BlockSpec((None, SLC_LEN, None, G, d), cache_map(n, 3)) for n in range(n_sel)]
    out = pl.pallas_call(
        functools.partial(_slc_s_kernel, n_sel=n_sel, npb=npb, nt=nt, past=past, T=T),
        grid_spec=pltpu.PrefetchScalarGridSpec(
            num_scalar_prefetch=3,
            grid=(DB, G, T),
            in_specs=in_specs,
            out_specs=pl.BlockSpec((None, HPAD, d), lambda b, g, t, *_: ((b * G + g) * T + t, 0, 0)),
        ),
        out_shape=jax.ShapeDtypeStruct((DB * G * T, HPAD, d), f32),
        compiler_params=_cparams(("arbitrary", "arbitrary", "arbitrary")),
        name="slc_sample",
    )(sel.reshape(-1), page_table.reshape(-1).astype(jnp.int32), tsel.reshape(-1), q8, tail, tail,
      *([cache_kv] * (2 * n_sel)))
    return out.reshape(DB, G, T, HPAD, d)[:, :, :, :H].transpose(0, 2, 1, 3, 4)


def _nsa_sample(q, qr, pos, rows, win_keys, kc, vc, cache_kv, page_table):
    DB, T = q.shape[:2]
    past = page_table.shape[1] * PAGE_SIZE
    sub = PAGE_SIZE // SLC_LEN
    npb = past // SLC_LEN
    ns = -(-(past + T) // SLC_LEN)
    nt = ns - npb
    tail = jnp.pad(rows, ((0, 0), (0, nt * SLC_LEN - T), (0, 0), (0, 0), (0, 0)))
    o_cmp, p_cmp = _cmp_attention_s(q, kc, vc, pos)
    nc = p_cmp.shape[-1]
    score = jnp.einsum('bghqc,cj->bgqj', p_cmp, _overlap(nc, ns).astype(f32))
    sel = _select(score, pos, ns).astype(jnp.int32)
    o_slc = _slc_sample(qr, sel, tail, cache_kv, page_table, past)
    wb = win_keys.shape[1] - T
    kpos_w = past - wb + jnp.arange(wb + T)
    wmask = (kpos_w[None, :] <= pos[:, None]) & (kpos_w[None, :] >= pos[:, None] - WINDOW)
    o_win = _attend_s(qr, win_keys[:, :, 0], win_keys[:, :, 1], wmask)
    return o_cmp, o_slc, o_win


def kernel(x_prompt, x_sample, cache_kv, cache_win, state_ret, page_table,
           w_ret_in, w_ret_out, w_kv, w_cmp_k1, w_cmp_k2, pe_cmp_k, w_cmp_v1, w_cmp_v2, pe_cmp_v,
           w_nsa_in, w_nsa_out, w_router, b_router, w_exp_gate, w_exp_up, w_exp_down, ln_gain, ln_bias):
    B, S, D = x_prompt.shape
    DB, T, _ = x_sample.shape
    past = page_table.shape[1] * PAGE_SIZE
    wb = cache_win.shape[1]
    G, HPG, hd = NSA_KV, NSA_HPG, NSA_HD
    P = B * S
    Ttot = P + DB * TP
    pos_p = jnp.arange(S, dtype=jnp.int32)
    pos_sp = past + jnp.arange(TP, dtype=jnp.int32)
    pos_s = pos_sp[:T]
    pos_flat = jnp.concatenate([jnp.tile(pos_p, B), jnp.tile(pos_sp, DB)])
    chunk = min(RET_CHUNK, S)

    xs_pad = jnp.pad(x_sample, ((0, 0), (0, TP - T), (0, 0)))
    xt = jnp.concatenate([x_prompt.reshape(P, D), xs_pad.reshape(DB * TP, D)], axis=0)
    xt_b = xt.astype(bf16)

    ret_p, ret_s = [], []
    for l in range(DEPTH):
        if l < N_A_LAYERS:
            proj = _mm(xt_b, w_ret_in, layer=l)
            s0 = jnp.zeros((B, RET_HEADS, RET_DK, RET_DV), f32)
            o, sp = _retention(proj, pos_p[:chunk * (S // chunk)], s0, B, S // chunk, chunk, float(chunk), 0, None)
            o, ss = _retention(proj, pos_sp, state_ret[l].astype(f32), DB, 1, TP, float(T), P, o)
            ret_p.append(sp.astype(state_ret.dtype))
            ret_s.append(ss.astype(state_ret.dtype))
            y = _mm(o, w_ret_out, layer=l, tn=512)
        else:
            if l == N_A_LAYERS:
                kv = _mm(xt_b, w_kv)
                gd = G * hd
                rows = jnp.concatenate([kv[:, :2 * gd], _rope_heads(kv[:, 2 * gd:3 * gd], pos_flat),
                                        kv[:, 3 * gd:4 * gd]], axis=1)
                win_rows = jnp.concatenate([_rope_heads(kv[:, 4 * gd:5 * gd], pos_flat), kv[:, 5 * gd:]], axis=1)
                rows_p = rows[:P].reshape(B, S, 4, G, hd)
                rows_s = rows[P:].reshape(DB, TP, 4, G, hd)[:, :T]
                win_rows_s = win_rows[P:].reshape(DB, TP, 2, G, hd)[:, :T]
                win_keys_s = jnp.concatenate([cache_win, win_rows_s.astype(cache_win.dtype)], axis=1)

                def chunks_of(r):
                    n, N = r.shape[:2]
                    nch = N // CMP_STRIDE
                    c = r[:, :nch * CMP_STRIDE].reshape(n, nch, CMP_STRIDE, G, hd)
                    return c.transpose(0, 3, 1, 2, 4).reshape(n * G, nch, CMP_STRIDE * hd)

                def comp(r, w1, w2, pe):
                    n = r.shape[0]
                    c = _compress(chunks_of(r), w1, w2, pe)
                    return c.reshape(n, G, c.shape[1], hd).transpose(0, 2, 1, 3)

                kc_p = comp(rows_p[:, :, 0], w_cmp_k1, w_cmp_k2, pe_cmp_k)
                vc_p = comp(rows_p[:, :, 1], w_cmp_v1, w_cmp_v2, pe_cmp_v)
                assert (past + T) // CMP_STRIDE == past // CMP_STRIDE and page_table.shape[1] % CMP_P == 0
                kc_s, vc_s = _compress_cache(cache_kv, page_table, w_cmp_k1, w_cmp_k2, pe_cmp_k,
                                             w_cmp_v1, w_cmp_v2, pe_cmp_v)
                nc_p = kc_p.shape[1]
                ncp = -(-nc_p // 128) * 128
                ns_p = -(-S // SLC_LEN)

                def pad_c(c):
                    c = jnp.pad(c, ((0, 0), (0, ncp - nc_p), (0, 0), (0, 0)))
                    return c.transpose(0, 2, 1, 3).reshape(B * G, ncp, hd)

                kc_pp, vc_pp = pad_c(kc_p), pad_c(vc_p)
                ov_p = _overlap(ncp, ns_p).astype(bf16)
            li = l - N_A_LAYERS
            hq = NSA_HEADS * hd
            w_in = w_nsa_in[li]
            q = _mm(xt_b, w_in[:, :hq])
            gates = jax.nn.sigmoid(_mm(xt_b, w_in[:, hq:]))
            o_cmp, score = _cmp_prompt(q, kc_pp, vc_pp, ov_p, B, S)
            sel = _select(score.reshape(B, G, S, ns_p), pos_p, ns_p)
            selmask = jnp.sum(jax.nn.one_hot(sel, ns_p, dtype=f32), axis=-2).reshape(B * G, S, ns_p)
            gates_g = gates.reshape(Ttot, G, 3 * HPG).transpose(1, 0, 2)
            merged = _slc_win_prompt(q, pos_p, selmask, rows, win_rows, o_cmp, gates_g, B, S)
            q_s = q[P:].reshape(DB, TP, G, HPG, hd)[:, :T]
            qr_s = _rope_heads(q[P:], pos_flat[P:]).reshape(DB, TP, G, HPG, hd)[:, :T]
            oc_s, os_s, ow_s = _nsa_sample(q_s, qr_s, pos_s, rows_s, win_keys_s, kc_s, vc_s, cache_kv, page_table)
            g_s = gates[P:].reshape(DB, TP, G, HPG, 3)[:, :T]
            m_s = g_s[..., 0:1] * oc_s + g_s[..., 1:2] * os_s + g_s[..., 2:3] * ow_s
            m_s = jnp.pad(m_s.reshape(DB, T, hq), ((0, 0), (0, TP - T), (0, 0))).reshape(DB * TP, hq)
            merged = lax.dynamic_update_slice(merged, m_s.astype(merged.dtype), (P, 0))
            y = _mm(merged, w_nsa_out, layer=li)
        xt, xt_b = _post_norm(xt, y, ln_gain[l, 0], ln_bias[l, 0])
        ya, yb, wk = _moe(xt, xt_b, w_router, b_router, w_exp_gate, w_exp_up, w_exp_down, l)
        xt, xt_b = _post_norm_combine(xt, ya, yb, wk, ln_gain[l, 1], ln_bias[l, 1])

    xp = xt[:P].reshape(B, S, D)
    xs = xt[P:].reshape(DB, TP, D)[:, :T]
    win_rows_p = win_rows[:P].reshape(B, S, 2, G, hd)
    if S >= wb:
        win_p = win_rows_p[:, S - wb:]
    else:
        win_p = jnp.pad(win_rows_p, ((0, 0), (wb - S, 0), (0, 0), (0, 0), (0, 0)))
    win_s = win_keys_s[:, T:]
    return (xp, xs, rows_p, rows_s, win_p, win_s, jnp.stack(ret_p), jnp.stack(ret_s))
```

```python
import functools

import jax
import jax.numpy as jnp
from jax import lax
from jax.experimental import pallas as pl
from jax.experimental.pallas import tpu as pltpu

f32 = jnp.float32
bf16 = jnp.bfloat16

D_MODEL = 2048
DEPTH = 4
PAGE_SIZE = 128
N_A_LAYERS = DEPTH // 2
RET_HEADS = 8
RET_DK = D_MODEL // RET_HEADS
RET_DV = 2 * RET_DK
RET_CHUNK = 128
RET_THETA = 10000.0
NSA_HEADS = 16
NSA_HD = D_MODEL // NSA_HEADS
NSA_KV = 4
NSA_HPG = NSA_HEADS // NSA_KV
CMP_LEN = 32
CMP_STRIDE = 16
SLC_LEN = 64
N_SEL = 16
WINDOW = 512
QBLK = 128
ROPE_THETA = 500000.0
ROT_DIM = NSA_HD // 4
N_KV_PROJ = 6
N_EXPERTS = 16
N_GROUPS = 4
EXPERTS_PER_GROUP = N_EXPERTS // N_GROUPS
TOPK_GROUPS = 1
TOP_K = 2
D_EXPERT = 1408
DN_ALPHA = (2.0 * DEPTH) ** 0.25
LN_EPS = 1e-5

TP = 16
MOE_TM = 512
MOE_TF = 128
VMEM_LIMIT = 56 * 1024 * 1024
NEG = -1e30


def _cparams(sem):
    return pltpu.CompilerParams(dimension_semantics=sem, vmem_limit_bytes=VMEM_LIMIT)


def _bdot(a, b):
    return jnp.dot(a.astype(bf16), b.astype(bf16), preferred_element_type=f32)


def _bdot_nt(a, b):
    return lax.dot_general(a.astype(bf16), b.astype(bf16), (((1,), (1,)), ((), ())), preferred_element_type=f32)


def _mm_kernel(x_ref, w_ref, o_ref, wb_scr):
    @pl.when(pl.program_id(1) == 0)
    def _():
        wb_scr[...] = w_ref[...].astype(bf16)

    o_ref[...] = jnp.dot(x_ref[...].astype(bf16), wb_scr[...], preferred_element_type=f32)


def _mm3_kernel(x_ref, w_ref, o_ref, wb_scr):
    x = x_ref[...]
    w = w_ref[...]
    xh = x.astype(bf16)
    wh = w.astype(bf16)
    xl = (x - xh.astype(f32)).astype(bf16)
    wl = (w - wh.astype(f32)).astype(bf16)
    acc = jnp.dot(xh, wh, preferred_element_type=f32)
    acc += jnp.dot(xh, wl, preferred_element_type=f32)
    acc += jnp.dot(xl, wh, preferred_element_type=f32)
    o_ref[...] = acc


def _row_tile(m, pref=640):
    for t in (pref, 512, 256, 128):
        if m % t == 0:
            return t
    return min(512, m)


def _mm(x, w, layer=None, tn=1024, body=_mm_kernel):
    M, K = x.shape
    N = w.shape[-1]
    tn = min(tn, N)
    tm = _row_tile(M)
    if layer is None:
        w_spec = pl.BlockSpec((K, tn), lambda j, i: (0, j))
    else:
        w_spec = pl.BlockSpec((None, K, tn), lambda j, i: (layer, 0, j))
    return pl.pallas_call(
        body,
        grid=(pl.cdiv(N, tn), pl.cdiv(M, tm)),
        in_specs=[pl.BlockSpec((tm, K), lambda j, i: (i, 0)), w_spec],
        out_specs=pl.BlockSpec((tm, tn), lambda j, i: (i, j)),
        out_shape=jax.ShapeDtypeStruct((M, N), f32),
        scratch_shapes=[pltpu.VMEM((K, tn), bf16)],
        compiler_params=_cparams(("parallel", "arbitrary")),
        name="mm",
    )(x, w)


def _layer_norm(h, g, b):
    mu = jnp.mean(h, -1, keepdims=True)
    d = h - mu
    var = jnp.mean(d * d, -1, keepdims=True)
    return d * lax.rsqrt(var + LN_EPS) * g + b


def _pn_kernel(x_ref, y_ref, g_ref, b_ref, o_ref):
    o_ref[...] = _layer_norm(DN_ALPHA * x_ref[...] + y_ref[...], g_ref[...], b_ref[...])


def _pn_combine_kernel(x_ref, ya_ref, yb_ref, w_ref, g_ref, b_ref, o_ref, ob_ref):
    w = w_ref[...]
    y = ya_ref[...] * w[:, 0:1] + yb_ref[...] * w[:, 1:2]
    o = _layer_norm(DN_ALPHA * x_ref[...] + y, g_ref[...], b_ref[...])
    o_ref[...] = o
    ob_ref[...] = o.astype(bf16)


def _post_norm(x, y, g, b):
    M, D = x.shape
    tm = _row_tile(M, 128)
    row = pl.BlockSpec((tm, D), lambda i: (i, 0))
    vec = pl.BlockSpec((1, D), lambda i: (0, 0))
    return pl.pallas_call(
        _pn_kernel,
        grid=(pl.cdiv(M, tm),),
        in_specs=[row, row, vec, vec],
        out_specs=row,
        out_shape=jax.ShapeDtypeStruct((M, D), f32),
        compiler_params=_cparams(("parallel",)),
        name="post_norm",
    )(x, y, g.reshape(1, D), b.reshape(1, D))


def _post_norm_combine(x, ya, yb, w, g, b):
    M, D = x.shape
    tm = _row_tile(M, 128)
    row = pl.BlockSpec((tm, D), lambda i: (i, 0))
    vec = pl.BlockSpec((1, D), lambda i: (0, 0))
    return pl.pallas_call(
        _pn_combine_kernel,
        grid=(pl.cdiv(M, tm),),
        in_specs=[row, row, row, pl.BlockSpec((tm, TOP_K), lambda i: (i, 0)), vec, vec],
        out_specs=[row, row],
        out_shape=[jax.ShapeDtypeStruct((M, D), f32), jax.ShapeDtypeStruct((M, D), bf16)],
        compiler_params=_cparams(("parallel",)),
        name="post_norm_combine",
    )(x, ya, yb, w, g.reshape(1, D), b.reshape(1, D))


RET_HB = 2


def _ret_kernel(q_ref, k_ref, v_ref, g_ref, cos_ref, sin_ref, intra_ref, qd_ref, kd_ref, cd_ref, s0_ref,
                o_ref, sfin_ref, s_scr, *, nc):
    c = pl.program_id(2)

    @pl.when(c == 0)
    def _():
        s_scr[...] = s0_ref[...]

    half = RET_DK // 2
    cos = cos_ref[...]
    sin = sin_ref[...]

    def rot(t):
        t1, t2 = t[:, :half], t[:, half:]
        return jnp.concatenate([t1 * cos - t2 * sin, t2 * cos + t1 * sin], axis=-1)

    for hh in range(RET_HB):
        kcol = slice(hh * RET_DK, (hh + 1) * RET_DK)
        vcol = slice(hh * RET_DV, (hh + 1) * RET_DV)
        q = rot(q_ref[:, kcol])
        k = rot(k_ref[:, kcol]) * (RET_DK ** -0.5)
        v = v_ref[:, vcol].astype(bf16)
        s = s_scr[hh]
        att = _bdot_nt(q, k) * intra_ref[hh]
        o = _bdot(att, v) + _bdot(q * qd_ref[hh], s)
        kd = (k * kd_ref[hh]).T
        s_scr[hh] = s * cd_ref[hh] + _bdot(kd, v)
        mu = jnp.mean(o, -1, keepdims=True)
        d = o - mu
        var = jnp.mean(d * d, -1, keepdims=True)
        gate = g_ref[:, vcol]
        o_ref[:, vcol] = (gate * jax.nn.sigmoid(gate) * (d * lax.rsqrt(var + LN_EPS))).astype(o_ref.dtype)

    @pl.when(c == nc - 1)
    def _():
        sfin_ref[...] = s_scr[...]


def _ret_decay(chunk, valid):
    log_g = jnp.log(1.0 - 2.0 ** (-5.0 - jnp.arange(RET_HEADS, dtype=f32)))
    i = jnp.arange(chunk, dtype=f32)
    real = i < valid
    diff = i[:, None] - i[None, :]
    intra = jnp.where((diff >= 0) & real[None, :], jnp.exp(log_g[:, None, None] * jnp.maximum(diff, 0.0)), 0.0)
    q_dec = jnp.exp(log_g[:, None] * (i + 1.0))[:, :, None]
    k_dec = jnp.where(real, jnp.exp(log_g[:, None] * jnp.maximum(valid - 1.0 - i, 0.0)), 0.0)[:, :, None]
    c_dec = jnp.exp(log_g * valid)[:, None, None]
    return intra, q_dec, k_dec, c_dec


def _ret_tables(pos):
    inv = 1.0 / (RET_THETA ** jnp.linspace(0.0, 1.0, RET_DK // 2, dtype=f32))
    ang = pos.astype(f32)[:, None] * inv[None, :]
    return jnp.cos(ang), jnp.sin(ang)


def _retention(proj, pos, s0, nb, nc, chunk, valid, row0, o_prev):
    Ttot = proj.shape[0]
    hk, hv = RET_HEADS * RET_DK, RET_HEADS * RET_DV
    rb0 = row0 // chunk
    cos, sin = _ret_tables(pos)
    intra, q_dec, k_dec, c_dec = _ret_decay(chunk, valid)
    H = RET_HEADS

    def rowblk(b, h, c):
        return rb0 + b * nc + c

    HB = RET_HB
    wk, wv = HB * RET_DK, HB * RET_DV
    in_specs = [
        pl.BlockSpec((chunk, wk), lambda b, h, c: (rowblk(b, h, c), h)),
        pl.BlockSpec((chunk, wk), lambda b, h, c: (rowblk(b, h, c), hk // wk + h)),
        pl.BlockSpec((chunk, wv), lambda b, h, c: (rowblk(b, h, c), (2 * hk) // wv + h)),
        pl.BlockSpec((chunk, wv), lambda b, h, c: (rowblk(b, h, c), (2 * hk + hv) // wv + h)),
        pl.BlockSpec((chunk, RET_DK // 2), lambda b, h, c: (c, 0)),
        pl.BlockSpec((chunk, RET_DK // 2), lambda b, h, c: (c, 0)),
        pl.BlockSpec((HB, chunk, chunk), lambda b, h, c: (h, 0, 0)),
        pl.BlockSpec((HB, chunk, 1), lambda b, h, c: (h, 0, 0)),
        pl.BlockSpec((HB, chunk, 1), lambda b, h, c: (h, 0, 0)),
        pl.BlockSpec((HB, 1, 1), lambda b, h, c: (h, 0, 0)),
        pl.BlockSpec((None, HB, RET_DK, RET_DV), lambda b, h, c: (b, h, 0, 0)),
    ]
    args = [proj, proj, proj, proj, cos, sin, intra, q_dec, k_dec, c_dec, s0]
    aliases = {}
    if o_prev is not None:
        in_specs.append(pl.BlockSpec(memory_space=pl.ANY))
        args.append(o_prev)
        aliases = {len(args) - 1: 0}

    def body(*refs):
        if o_prev is not None:
            refs = refs[:11] + refs[12:]
        _ret_kernel(*refs, nc=nc)

    return pl.pallas_call(
        body,
        grid=(nb, H // HB, nc),
        in_specs=in_specs,
        out_specs=[
            pl.BlockSpec((chunk, wv), lambda b, h, c: (rowblk(b, h, c), h)),
            pl.BlockSpec((None, HB, RET_DK, RET_DV), lambda b, h, c: (b, h, 0, 0)),
        ],
        out_shape=[jax.ShapeDtypeStruct((Ttot, hv), bf16), jax.ShapeDtypeStruct((nb, H, RET_DK, RET_DV), f32)],
        scratch_shapes=[pltpu.VMEM((HB, RET_DK, RET_DV), f32)],
        input_output_aliases=aliases,
        compiler_params=_cparams(("parallel", "parallel", "arbitrary")),
        name="retention",
    )(*args)


def _moe_kernel(be_ref, nu_ref, src_ref, x_hbm, wga_ref, wgb_ref, wua_ref, wub_ref, wda_ref, wdb_ref, o_ref,
                xg_scr, xb_scr, sem, *, nf):
    i = pl.program_id(0)
    j = pl.program_id(1)
    n_used = nu_ref[0]

    def gather_rows(blk, slot):
        def start(r, carry):
            row = src_ref[blk * MOE_TM + r]
            pltpu.make_async_copy(x_hbm.at[pl.ds(row, 1)], xg_scr.at[slot, pl.ds(r, 1)], sem.at[slot]).start()
            return carry
        lax.fori_loop(0, MOE_TM, start, 0, unroll=8)

    @pl.when((i < n_used) & (j == 0))
    def _():
        slot = i % 2

        @pl.when(i == 0)
        def _():
            gather_rows(0, 0)

        pltpu.make_async_copy(x_hbm.at[pl.ds(0, MOE_TM)], xg_scr.at[slot], sem.at[slot]).wait()

        @pl.when(i + 1 < n_used)
        def _():
            gather_rows(i + 1, 1 - slot)

        xb_scr[...] = xg_scr[slot].astype(bf16)
        o_ref[...] = jnp.zeros(o_ref.shape, f32)

    @pl.when(i < n_used)
    def _():
        xb = xb_scr[...]
        wg = jnp.concatenate([wga_ref[...].astype(bf16), wgb_ref[...].astype(bf16)], axis=1)
        wu = jnp.concatenate([wua_ref[...].astype(bf16), wub_ref[...].astype(bf16)], axis=1)
        wd = jnp.concatenate([wda_ref[...].astype(bf16), wdb_ref[...].astype(bf16)], axis=0)
        g = jnp.dot(xb, wg, preferred_element_type=f32)
        u = jnp.dot(xb, wu, preferred_element_type=f32)
        h = g * jax.nn.sigmoid(g) * u
        live = jnp.where(2 * j + 1 < nf, 2 * MOE_TF, MOE_TF)
        h = jnp.where(lax.broadcasted_iota(jnp.int32, h.shape, 1) < live, h, 0.0)
        o_ref[...] += jnp.dot(h.astype(bf16), wd, preferred_element_type=f32)


def _moe_experts(x, rows_src, blk_e, n_used, w_g, w_u, w_d, layer):
    D = x.shape[1]
    R = rows_src.shape[0]
    n_blk = R // MOE_TM
    nf = D_EXPERT // MOE_TF
    nsteps = -(-nf // 2)

    def ii(i, nu):
        return jnp.minimum(i, nu[0] - 1)

    def jj(i, j, nu, odd):
        return jnp.minimum(2 * jnp.where(i < nu[0], j, nsteps - 1) + odd, nf - 1)

    def w_in_spec(odd):
        return pl.BlockSpec((None, None, D, MOE_TF),
                            lambda i, j, be, nu, src: (layer, be[ii(i, nu)], 0, jj(i, j, nu, odd)))

    def w_out_spec(odd):
        return pl.BlockSpec((None, None, MOE_TF, D),
                            lambda i, j, be, nu, src: (layer, be[ii(i, nu)], jj(i, j, nu, odd), 0))

    grid_spec = pltpu.PrefetchScalarGridSpec(
        num_scalar_prefetch=3,
        grid=(n_blk, nsteps),
        in_specs=[
            pl.BlockSpec(memory_space=pl.ANY),
            w_in_spec(0), w_in_spec(1), w_in_spec(0), w_in_spec(1), w_out_spec(0), w_out_spec(1),
        ],
        out_specs=pl.BlockSpec((MOE_TM, D), lambda i, j, be, nu, src: (ii(i, nu), 0)),
        scratch_shapes=[
            pltpu.VMEM((2, MOE_TM, D), f32),
            pltpu.VMEM((MOE_TM, D), bf16),
            pltpu.SemaphoreType.DMA((2,)),
        ],
    )
    return pl.pallas_call(
        functools.partial(_moe_kernel, nf=nf),
        grid_spec=grid_spec,
        out_shape=jax.ShapeDtypeStruct((R, D), f32),
        compiler_params=_cparams(("arbitrary", "arbitrary")),
        name="moe_experts",
    )(blk_e, n_used, rows_src, x, w_g, w_g, w_u, w_u, w_d, w_d)


def _topk_small(x, k):
    lane = jnp.arange(x.shape[-1])
    vals, idxs = [], []
    for _ in range(k):
        i = jnp.argmax(x, axis=-1)
        vals.append(jnp.max(x, axis=-1))
        idxs.append(i)
        x = jnp.where(lane == i[..., None], -jnp.inf, x)
    return jnp.stack(vals, -1), jnp.stack(idxs, -1)


def _moe(x, w_router, b_router, w_g, w_u, w_d, layer):
    T, D = x.shape
    logits = _mm(x, w_router, body=_mm3_kernel)
    aff = jax.nn.sigmoid(logits)
    biased = aff + b_router.astype(f32)
    grp = biased.reshape(T, N_GROUPS, EXPERTS_PER_GROUP)
    grp_score = _topk_small(grp, 2)[0].sum(-1)
    _, top_g = _topk_small(grp_score, TOPK_GROUPS)
    g_mask = jnp.any(top_g[:, :, None] == jnp.arange(N_GROUPS)[None, None, :], axis=1)
    e_mask = jnp.repeat(g_mask, EXPERTS_PER_GROUP, axis=1)
    _, idx = _topk_small(jnp.where(e_mask, biased, -jnp.inf), TOP_K)
    w = jnp.take_along_axis(aff, idx, axis=1)
    w = w / jnp.sum(w, -1, keepdims=True)
    A = T * TOP_K
    e_flat = idx.reshape(A).astype(jnp.int32)
    order = jnp.argsort(e_flat)
    e_s = e_flat[order]
    counts = jnp.bincount(e_flat, length=N_EXPERTS)
    start = jnp.cumsum(counts) - counts
    padded = (counts + MOE_TM - 1) // MOE_TM * MOE_TM
    pend = jnp.cumsum(padded)
    pstart = pend - padded
    dest = (pstart[e_s] + (jnp.arange(A) - start[e_s])).astype(jnp.int32)
    n_blk = -(-A // MOE_TM) + N_EXPERTS
    rows_src = (jnp.arange(n_blk * MOE_TM, dtype=jnp.int32) % T).at[dest].set((order // TOP_K).astype(jnp.int32))
    blk_e = jnp.minimum(jnp.searchsorted(pend, jnp.arange(n_blk) * MOE_TM, side='right'), N_EXPERTS - 1)
    n_used = (pend[-1] // MOE_TM).astype(jnp.int32).reshape(1)
    out = _moe_experts(x, rows_src, blk_e.astype(jnp.int32), n_used, w_g, w_u, w_d, layer)
    pos = jnp.zeros((A,), jnp.int32).at[order].set(dest).reshape(T, TOP_K)
    return out[pos[:, 0]], out[pos[:, 1]], w


def _nsa_inv_freq():
    return ROPE_THETA ** (-jnp.arange(0, ROT_DIM, 2, dtype=f32) / ROT_DIM)


def _rope_heads(x, pos):
    T = x.shape[0]
    half = ROT_DIM // 2
    ang = pos.astype(f32)[:, None] * _nsa_inv_freq()[None, :]
    cos = jnp.cos(ang)[:, None, :]
    sin = jnp.sin(ang)[:, None, :]
    xh = x.reshape(T, -1, NSA_HD)
    x1, x2 = xh[..., :half], xh[..., half:ROT_DIM]
    out = jnp.concatenate([x1 * cos - x2 * sin, x2 * cos + x1 * sin, xh[..., ROT_DIM:]], axis=-1)
    return out.reshape(x.shape)


def _masked_softmax(s, mask):
    s = jnp.where(mask, s, NEG)
    m = jnp.max(s, axis=-1, keepdims=True)
    m = jnp.where(m <= 0.5 * NEG, 0.0, m)
    e = jnp.where(mask, jnp.exp(s - m), 0.0)
    den = jnp.sum(e, axis=-1, keepdims=True)
    return e / jnp.where(den > 0, den, 1.0)


def _compress(chunks, w1, w2, pe):
    n, nchunk, kk = chunks.shape
    wab = jnp.concatenate([w1[:kk], w1[kk:]], axis=1)
    y = _mm(chunks.reshape(n * nchunk, kk), wab).reshape(n, nchunk, 2 * NSA_HD)
    bias = _mm(jnp.broadcast_to(pe.reshape(1, CMP_LEN * NSA_HD), (8, CMP_LEN * NSA_HD)), w1)[:1]
    pre = y[:, :-1, :NSA_HD] + y[:, 1:, NSA_HD:] + bias
    nc = nchunk - 1
    return _mm(jax.nn.silu(pre).reshape(n * nc, NSA_HD), w2).reshape(n, nc, NSA_HD)


CMP_P = 8


def _cmp_pages_kernel(pt_ref, *refs):
    P = CMP_P
    page_refs, w_ref, o_ref, x_scr = refs[:2 * P], refs[2 * P], refs[2 * P + 1], refs[2 * P + 2]
    cpp = PAGE_SIZE // CMP_STRIDE
    for typ in range(2):
        for r in range(CMP_STRIDE):
            for g in range(NSA_KV):
                for p in range(P):
                    row = (g * P + p) * cpp
                    x_scr[row:row + cpp, r * NSA_HD:(r + 1) * NSA_HD] = (
                        page_refs[typ * P + p][pl.ds(r, cpp, stride=CMP_STRIDE), g, :])
        y = _bdot(x_scr[...], w_ref[typ])
        for g in range(NSA_KV):
            o_ref[typ, g] = y[g * P * cpp:(g + 1) * P * cpp]


def _cmp_pages(cache_kv, page_table, wab):
    DB, n_pages = page_table.shape
    P = CMP_P
    G, d = NSA_KV, NSA_HD
    cpp = PAGE_SIZE // CMP_STRIDE
    kk = CMP_STRIDE * d

    def page_map(p, typ):
        return lambda b, j, pt: (pt[b * n_pages + j * P + p], 0, typ, 0, 0)

    in_specs = [pl.BlockSpec((None, PAGE_SIZE, None, G, d), page_map(p, typ)) for typ in range(2) for p in range(P)]
    in_specs.append(pl.BlockSpec((2, kk, 2 * d), lambda b, j, pt: (0, 0, 0)))
    return pl.pallas_call(
        _cmp_pages_kernel,
        grid_spec=pltpu.PrefetchScalarGridSpec(
            num_scalar_prefetch=1,
            grid=(DB, n_pages // P),
            in_specs=in_specs,
            out_specs=pl.BlockSpec((None, 2, G, P * cpp, 2 * d), lambda b, j, pt: (b, 0, 0, j, 0)),
            scratch_shapes=[pltpu.VMEM((G * P * cpp, kk), f32)],
        ),
        out_shape=jax.ShapeDtypeStruct((DB, 2, G, n_pages * cpp, 2 * d), f32),
        compiler_params=_cparams(("arbitrary", "arbitrary")),
        name="cmp_pages",
    )(page_table.reshape(-1).astype(jnp.int32), *([cache_kv] * (2 * P)), wab)


def _compress_cache(cache_kv, page_table, w_k1, w_k2, pe_k, w_v1, w_v2, pe_v):
    DB = page_table.shape[0]
    G, d = NSA_KV, NSA_HD
    kk = CMP_STRIDE * d
    wab = jnp.stack([jnp.concatenate([w[:kk], w[kk:]], axis=1) for w in (w_k1, w_v1)])
    y = _cmp_pages(cache_kv, page_table, wab)
    out = []
    for typ, (w1, w2, pe) in enumerate(((w_k1, w_k2, pe_k), (w_v1, w_v2, pe_v))):
        bias = _mm(jnp.broadcast_to(pe.reshape(1, CMP_LEN * d), (8, CMP_LEN * d)), w1)[:1]
        pre = y[:, typ, :, :-1, :d] + y[:, typ, :, 1:, d:] + bias
        nc = pre.shape[2]
        c = _mm(jax.nn.silu(pre).reshape(DB * G * nc, d), w2)
        out.append(c.reshape(DB, G, nc, d).transpose(0, 2, 1, 3))
    return out


def _cmp_kernel(q_ref, kc_ref, vc_ref, ov_ref, o_ref, sc_ref):
    qi = pl.program_id(2)
    ncp = kc_ref.shape[0]
    H, d = NSA_HPG, NSA_HD
    rows = lax.broadcasted_iota(jnp.int32, (H * QBLK, ncp), 0)
    qpos = qi * QBLK + rows % QBLK
    cend = lax.broadcasted_iota(jnp.int32, (H * QBLK, ncp), 1) * CMP_STRIDE + (CMP_LEN - 1)
    mask = cend <= qpos
    kc = kc_ref[...].astype(bf16)
    vc = vc_ref[...].astype(bf16)
    q4 = jnp.concatenate([q_ref[:, h * d:(h + 1) * d].astype(bf16) for h in range(H)], axis=0)
    s = _bdot_nt(q4, kc) * (d ** -0.5)
    p = _masked_softmax(s, mask).astype(bf16)
    o4 = jnp.dot(p, vc, preferred_element_type=f32)
    sc4 = jnp.dot(p, ov_ref[...], preferred_element_type=f32)
    for h in range(H):
        o_ref[:, h * d:(h + 1) * d] = o4[h * QBLK:(h + 1) * QBLK]
    sc_ref[...] = functools.reduce(jnp.add, [sc4[h * QBLK:(h + 1) * QBLK] for h in range(H)])


def _cmp_prompt(q, kc, vc, overlap, B, S):
    Ttot = q.shape[0]
    nq = S // QBLK
    ncp = kc.shape[1]
    ns = overlap.shape[1]
    G = NSA_KV
    gw = NSA_HPG * NSA_HD
    return pl.pallas_call(
        _cmp_kernel,
        grid=(B, G, nq),
        in_specs=[
            pl.BlockSpec((QBLK, gw), lambda b, g, i: (b * nq + i, g)),
            pl.BlockSpec((None, ncp, NSA_HD), lambda b, g, i: (b * G + g, 0, 0)),
            pl.BlockSpec((None, ncp, NSA_HD), lambda b, g, i: (b * G + g, 0, 0)),
            pl.BlockSpec((ncp, ns), lambda b, g, i: (0, 0)),
        ],
        out_specs=[
            pl.BlockSpec((QBLK, gw), lambda b, g, i: (b * nq + i, g)),
            pl.BlockSpec((None, QBLK, ns), lambda b, g, i: (b * G + g, i, 0)),
        ],
        out_shape=[jax.ShapeDtypeStruct((Ttot, G * gw), f32), jax.ShapeDtypeStruct((B * G, S, ns), f32)],
        compiler_params=_cparams(("parallel", "parallel", "parallel")),
        name="cmp_prompt",
    )(q, kc, vc, overlap)


SEL_TK = 512


LOG2E = 1.4426950408889634
LANES = 128


def _fold_lanes(x, op):
    parts = [x[:, j * LANES:(j + 1) * LANES] for j in range(x.shape[1] // LANES)]
    return functools.reduce(op, parts)


def _slc_win_kernel(q_ref, rc_ref, rs1_ref, rs2_ref, sel_ref, ks_ref, vs_ref, kw_ref, vw_ref, oc_ref, gt_ref, o_ref,
                    s_scr, m_scr, l_scr, acc_scr):
    qi = pl.program_id(2)
    H, d = NSA_HPG, NSA_HD
    c2 = (d ** -0.5) * LOG2E
    rc, rs1, rs2 = rc_ref[...], rs1_ref[...], rs2_ref[...]
    half = ROT_DIM // 2

    def roped(h):
        x = q_ref[:, h * d:(h + 1) * d]
        return x * rc + pltpu.roll(x, d - half, 1) * rs1 + pltpu.roll(x, half, 1) * rs2

    q4 = jnp.concatenate([(roped(h) * c2).astype(bf16) for h in range(H)], axis=0)
    sel = sel_ref[...].astype(bf16)
    ns = sel.shape[1]
    bpt = SEL_TK // SLC_LEN
    nl = SEL_TK // LANES

    m_scr[...] = jnp.full(m_scr.shape, NEG, f32)
    l_scr[...] = jnp.zeros(l_scr.shape, f32)
    acc_scr[...] = jnp.zeros(acc_scr.shape, f32)
    qpos = qi * QBLK + lax.broadcasted_iota(jnp.int32, (QBLK, SEL_TK), 0)
    n_tiles = (qi * QBLK + QBLK - 1) // SEL_TK + 1

    def scores(t, carry):
        k0 = pl.multiple_of(t * SEL_TK, SEL_TK)
        k = ks_ref[pl.ds(k0, SEL_TK), :].astype(bf16)
        blk = lax.broadcasted_iota(jnp.int32, (ns, SEL_TK), 1) // SLC_LEN + t * bpt
        expand = jnp.where(blk == lax.broadcasted_iota(jnp.int32, (ns, SEL_TK), 0), 1.0, 0.0).astype(bf16)
        chosen = jnp.dot(sel, expand, preferred_element_type=f32) > 0.5
        kpos = k0 + lax.broadcasted_iota(jnp.int32, (QBLK, SEL_TK), 1)
        bias = jnp.where(chosen & (kpos <= qpos), 0.0, NEG)
        s4 = _bdot_nt(q4, k)
        for h in range(H):
            rows = slice(h * QBLK, (h + 1) * QBLK)
            s = s4[rows] + bias
            s_scr[t, rows] = s
            m_scr[rows] = jnp.maximum(m_scr[rows], _fold_lanes(s, jnp.maximum))
        return carry

    lax.fori_loop(0, n_tiles, scores, 0)
    m_scr[...] = jnp.broadcast_to(jnp.max(m_scr[...], axis=-1, keepdims=True), m_scr.shape)

    def accumulate(t, carry):
        k0 = pl.multiple_of(t * SEL_TK, SEL_TK)
        v = vs_ref[pl.ds(k0, SEL_TK), :].astype(bf16)
        s = s_scr[t]
        m = m_scr[...]
        p = jnp.concatenate([jnp.exp2(s[:, j * LANES:(j + 1) * LANES] - m) for j in range(nl)], axis=-1)
        l_scr[...] += _fold_lanes(p, jnp.add)
        acc_scr[...] += jnp.dot(p.astype(bf16), v, preferred_element_type=f32)
        return carry

    lax.fori_loop(0, n_tiles, accumulate, 0)
    o_slc = acc_scr[...] / jnp.sum(l_scr[...], axis=-1, keepdims=True)

    wk = WINDOW + QBLK
    w0 = pl.multiple_of(jnp.maximum(qi * QBLK - WINDOW, 0), QBLK)
    kw = kw_ref[pl.ds(w0, wk), :].astype(bf16)
    vw = vw_ref[pl.ds(w0, wk), :].astype(bf16)
    qp = qi * QBLK + lax.broadcasted_iota(jnp.int32, (QBLK, wk), 0)
    kp = w0 + lax.broadcasted_iota(jnp.int32, (QBLK, wk), 1)
    wbias = jnp.where((kp <= qp) & (kp >= qp - WINDOW), 0.0, NEG)
    sw = _bdot_nt(q4, kw)
    sw = jnp.concatenate([sw[h * QBLK:(h + 1) * QBLK] + wbias for h in range(H)], axis=0)
    pw = jnp.exp2(sw - jnp.max(sw, axis=-1, keepdims=True))
    o_win = jnp.dot(pw.astype(bf16), vw, preferred_element_type=f32) / jnp.sum(pw, axis=-1, keepdims=True)

    gt = gt_ref[...]
    for h in range(H):
        rows = slice(h * QBLK, (h + 1) * QBLK)
        o = (gt[:, 3 * h:3 * h + 1] * oc_ref[:, h * d:(h + 1) * d]
             + gt[:, 3 * h + 1:3 * h + 2] * o_slc[rows] + gt[:, 3 * h + 2:3 * h + 3] * o_win[rows])
        o_ref[:, h * d:(h + 1) * d] = o.astype(o_ref.dtype)


def _rope_tables(pos):
    half = ROT_DIM // 2
    ang = pos.astype(f32)[:, None] * _nsa_inv_freq()[None, :]
    cos, sin = jnp.cos(ang), jnp.sin(ang)
    n = pos.shape[0]
    c = jnp.concatenate([cos, cos, jnp.ones((n, NSA_HD - ROT_DIM), f32)], axis=1)
    s1 = jnp.concatenate([-sin, jnp.zeros((n, NSA_HD - half), f32)], axis=1)
    s2 = jnp.concatenate([jnp.zeros((n, half), f32), sin, jnp.zeros((n, NSA_HD - ROT_DIM), f32)], axis=1)
    return c, s1, s2


def _slc_win_prompt(q, pos, selmask, rows, win_rows, o_cmp, gates, B, S):
    Ttot = q.shape[0]
    nq = S // QBLK
    ns = selmask.shape[-1]
    G = NSA_KV
    gw = NSA_HPG * NSA_HD
    rc, rs1, rs2 = _rope_tables(pos)
    qspec = pl.BlockSpec((QBLK, gw), lambda b, g, i: (b * nq + i, g))
    tspec = pl.BlockSpec((QBLK, NSA_HD), lambda b, g, i: (i, 0))
    return pl.pallas_call(
        _slc_win_kernel,
        grid=(B, G, nq),
        in_specs=[
            qspec, tspec, tspec, tspec,
            pl.BlockSpec((None, QBLK, ns), lambda b, g, i: (b * G + g, i, 0)),
            pl.BlockSpec((S, NSA_HD), lambda b, g, i: (b, 2 * G + g)),
            pl.BlockSpec((S, NSA_HD), lambda b, g, i: (b, 3 * G + g)),
            pl.BlockSpec((S, NSA_HD), lambda b, g, i: (b, g)),
            pl.BlockSpec((S, NSA_HD), lambda b, g, i: (b, G + g)),
            qspec,
            pl.BlockSpec((None, QBLK, 3 * NSA_HPG), lambda b, g, i: (g, b * nq + i, 0)),
        ],
        out_specs=qspec,
        out_shape=jax.ShapeDtypeStruct((Ttot, G * gw), bf16),
        scratch_shapes=[
            pltpu.VMEM((S // SEL_TK, NSA_HPG * QBLK, SEL_TK), f32),
            pltpu.VMEM((NSA_HPG * QBLK, LANES), f32),
            pltpu.VMEM((NSA_HPG * QBLK, LANES), f32),
            pltpu.VMEM((NSA_HPG * QBLK, NSA_HD), f32),
        ],
        compiler_params=_cparams(("parallel", "parallel", "arbitrary")),
        name="slc_win_prompt",
    )(q, rc, rs1, rs2, selmask, rows, rows, win_rows, win_rows, o_cmp, gates)


def _overlap(nc, ns):
    c0 = jnp.arange(nc) * CMP_STRIDE
    j0 = jnp.arange(ns) * SLC_LEN
    return ((c0[:, None] < j0[None, :] + SLC_LEN) & (c0[:, None] + CMP_LEN > j0[None, :]))


def _select(score, qpos, ns):
    tblk = qpos // SLC_LEN
    j = jnp.arange(ns)
    forced = (j[None, :] == 0) | (j[None, :] == tblk[:, None]) | (j[None, :] == tblk[:, None] - 1)
    score = jnp.where(forced, jnp.inf, jnp.where(j[None, :] > tblk[:, None], -jnp.inf, score))
    _, idx = lax.top_k(score, min(N_SEL, ns))
    return idx


def _cmp_attention_s(q, kc, vc, qpos):
    nc = kc.shape[1]
    cend = jnp.arange(nc) * CMP_STRIDE + (CMP_LEN - 1)
    mask = cend[None, :] <= qpos[:, None]
    s = jnp.einsum('bqghd,bcgd->bghqc', q, kc, preferred_element_type=f32) * (NSA_HD ** -0.5)
    p = _masked_softmax(s, mask)
    o = jnp.einsum('bghqc,bcgd->bqghd', p, vc)
    return o, p


def _attend_s(q, k, v, mask):
    s = jnp.einsum('bqghd,bkgd->bghqk', q, k, preferred_element_type=f32) * (NSA_HD ** -0.5)
    p = _masked_softmax(s, mask[None, None, None, :, :])
    return jnp.einsum('bghqk,bkgd->bqghd', p, v)


HPAD = 16


def _slc_s_kernel(sel_ref, pt_ref, tsel_ref, q_ref, tk_ref, tv_ref, *refs, n_sel, npb, nt, past, T):
    k_refs, v_refs, o_ref = refs[:n_sel], refs[n_sel:2 * n_sel], refs[2 * n_sel]
    k_scr, v_scr = refs[2 * n_sel + 1], refs[2 * n_sel + 2]
    b, g, t = pl.program_id(0), pl.program_id(1), pl.program_id(2)
    flat = (b * NSA_KV + g) * T + t
    qpos = past + t
    L = SLC_LEN
    nkeys = k_scr.shape[0]
    part = lax.broadcasted_iota(jnp.int32, (HPAD, nkeys), 1) // L
    lim = jnp.full((HPAD, nkeys), -1, jnp.int32)
    for n in range(n_sel):
        bid = sel_ref[flat * n_sel + n]
        k_scr[n * L:(n + 1) * L, :] = k_refs[n][:, g, :]
        v_scr[n * L:(n + 1) * L, :] = v_refs[n][:, g, :]
        lim = jnp.where(part == n, jnp.where(bid < npb, qpos - bid * L, -1), lim)
    for j in range(nt):
        n = n_sel + j
        k_scr[n * L:(n + 1) * L, :] = tk_ref[j * L:(j + 1) * L, g, :]
        v_scr[n * L:(n + 1) * L, :] = tv_ref[j * L:(j + 1) * L, g, :]
        lim = jnp.where(part == n, jnp.where(tsel_ref[flat * nt + j] > 0, qpos - (npb + j) * L, -1), lim)
    used = (n_sel + nt) * L
    if used < nkeys:
        k_scr[used:, :] = jnp.zeros((nkeys - used, NSA_HD), f32)
        v_scr[used:, :] = jnp.zeros((nkeys - used, NSA_HD), f32)
    off = lax.broadcasted_iota(jnp.int32, (HPAD, nkeys), 1) % L
    s = _bdot_nt(q_ref[...], k_scr[...]) * (NSA_HD ** -0.5)
    p = _masked_softmax(s, off <= lim)
    o_ref[...] = _bdot(p, v_scr[...])


def _slc_sample(qr, sel, tail, cache_kv, page_table, past):
    DB, T, G, H, d = qr.shape
    n_sel = sel.shape[-1]
    sub = PAGE_SIZE // SLC_LEN
    npb = past // SLC_LEN
    nt = tail.shape[1] // SLC_LEN
    n_pages = page_table.shape[1]
    nkeys = -(-((n_sel + nt) * SLC_LEN) // LANES) * LANES
    q8 = jnp.pad(qr.transpose(0, 2, 1, 3, 4), ((0, 0), (0, 0), (0, 0), (0, HPAD - H), (0, 0)))
    q8 = q8.reshape(DB * G * T, HPAD, d)
    tsel = jnp.any(sel[..., None] == npb + jnp.arange(nt), axis=-2).astype(jnp.int32)

    def cache_map(n, typ):
        def f(b, g, t, sel_r, pt_r, ts_r):
            bid = sel_r[((b * G + g) * T + t) * n_sel + n]
            jp = jnp.clip(bid, 0, npb - 1)
            return (pt_r[b * n_pages + jp // sub], jp % sub, typ, 0, 0)
        return f

    in_specs = [
        pl.BlockSpec((None, HPAD, d), lambda b, g, t, *_: ((b * G + g) * T + t, 0, 0)),
        pl.BlockSpec((None, nt * SLC_LEN, None, G, d), lambda b, g, t, *_: (b, 0, 2, 0, 0)),
        pl.BlockSpec((None, nt * SLC_LEN, None, G, d), lambda b, g, t, *_: (b, 0, 3, 0, 0)),
    ]
    in_specs += [pl.BlockSpec((None, SLC_LEN, None, G, d), cache_map(n, 2)) for n in range(n_sel)]
    in_specs += [pl.BlockSpec((None, SLC_LEN, None, G, d), cache_map(n, 3)) for n in range(n_sel)]
    out = pl.pallas_call(
        functools.partial(_slc_s_kernel, n_sel=n_sel, npb=npb, nt=nt, past=past, T=T),
        grid_spec=pltpu.PrefetchScalarGridSpec(
            num_scalar_prefetch=3,
            grid=(DB, G, T),
            in_specs=in_specs,
            out_specs=pl.BlockSpec((None, HPAD, d), lambda b, g, t, *_: ((b * G + g) * T + t, 0, 0)),
            scratch_shapes=[pltpu.VMEM((nkeys, d), f32), pltpu.VMEM((nkeys, d), f32)],
        ),
        out_shape=jax.ShapeDtypeStruct((DB * G * T, HPAD, d), f32),
        compiler_params=_cparams(("arbitrary", "arbitrary", "arbitrary")),
        name="slc_sample",
    )(sel.reshape(-1), page_table.reshape(-1).astype(jnp.int32), tsel.reshape(-1), q8, tail, tail,
      *([cache_kv] * (2 * n_sel)))
    return out.reshape(DB, G, T, HPAD, d)[:, :, :, :H].transpose(0, 2, 1, 3, 4)


def _nsa_sample(q, qr, pos, rows, win_keys, kc, vc, cache_kv, page_table):
    DB, T = q.shape[:2]
    past = page_table.shape[1] * PAGE_SIZE
    sub = PAGE_SIZE // SLC_LEN
    npb = past // SLC_LEN
    ns = -(-(past + T) // SLC_LEN)
    nt = ns - npb
    tail = jnp.pad(rows, ((0, 0), (0, nt * SLC_LEN - T), (0, 0), (0, 0), (0, 0)))
    o_cmp, p_cmp = _cmp_attention_s(q, kc, vc, pos)
    nc = p_cmp.shape[-1]
    score = jnp.einsum('bghqc,cj->bgqj', p_cmp, _overlap(nc, ns).astype(f32))
    sel = _select(score, pos, ns).astype(jnp.int32)
    o_slc = _slc_sample(qr, sel, tail, cache_kv, page_table, past)
    wb = win_keys.shape[1] - T
    kpos_w = past - wb + jnp.arange(wb + T)
    wmask = (kpos_w[None, :] <= pos[:, None]) & (kpos_w[None, :] >= pos[:, None] - WINDOW)
    o_win = _attend_s(qr, win_keys[:, :, 0], win_keys[:, :, 1], wmask)
    return o_cmp, o_slc, o_win


def kernel(x_prompt, x_sample, cache_kv, cache_win, state_ret, page_table,
           w_ret_in, w_ret_out, w_kv, w_cmp_k1, w_cmp_k2, pe_cmp_k, w_cmp_v1, w_cmp_v2, pe_cmp_v,
           w_nsa_in, w_nsa_out, w_router, b_router, w_exp_gate, w_exp_up, w_exp_down, ln_gain, ln_bias):
    B, S, D = x_prompt.shape
    DB, T, _ = x_sample.shape
    past = page_table.shape[1] * PAGE_SIZE
    wb = cache_win.shape[1]
    G, HPG, hd = NSA_KV, NSA_HPG, NSA_HD
    P = B * S
    Ttot = P + DB * TP
    pos_p = jnp.arange(S, dtype=jnp.int32)
    pos_sp = past + jnp.arange(TP, dtype=jnp.int32)
    pos_s = pos_sp[:T]
    pos_flat = jnp.concatenate([jnp.tile(pos_p, B), jnp.tile(pos_sp, DB)])
    chunk = min(RET_CHUNK, S)

    xs_pad = jnp.pad(x_sample, ((0, 0), (0, TP - T), (0, 0)))
    xt = jnp.concatenate([x_prompt.reshape(P, D), xs_pad.reshape(DB * TP, D)], axis=0)
    xt_b = xt.astype(bf16)

    ret_p, ret_s = [], []
    for l in range(DEPTH):
        if l < N_A_LAYERS:
            proj = _mm(xt_b, w_ret_in, layer=l)
            s0 = jnp.zeros((B, RET_HEADS, RET_DK, RET_DV), f32)
            o, sp = _retention(proj, pos_p[:chunk * (S // chunk)], s0, B, S // chunk, chunk, float(chunk), 0, None)
            o, ss = _retention(proj, pos_sp, state_ret[l].astype(f32), DB, 1, TP, float(T), P, o)
            ret_p.append(sp.astype(state_ret.dtype))
            ret_s.append(ss.astype(state_ret.dtype))
            y = _mm(o, w_ret_out, layer=l, tn=512)
        else:
            if l == N_A_LAYERS:
                kv = _mm(xt_b, w_kv)
                gd = G * hd
                rows = jnp.concatenate([kv[:, :2 * gd], _rope_heads(kv[:, 2 * gd:3 * gd], pos_flat),
                                        kv[:, 3 * gd:4 * gd]], axis=1)
                win_rows = jnp.concatenate([_rope_heads(kv[:, 4 * gd:5 * gd], pos_flat), kv[:, 5 * gd:]], axis=1)
                rows_p = rows[:P].reshape(B, S, 4, G, hd)
                rows_s = rows[P:].reshape(DB, TP, 4, G, hd)[:, :T]
                win_rows_s = win_rows[P:].reshape(DB, TP, 2, G, hd)[:, :T]
                win_keys_s = jnp.concatenate([cache_win, win_rows_s.astype(cache_win.dtype)], axis=1)

                def chunks_of(r):
                    n, N = r.shape[:2]
                    nch = N // CMP_STRIDE
                    c = r[:, :nch * CMP_STRIDE].reshape(n, nch, CMP_STRIDE, G, hd)
                    return c.transpose(0, 3, 1, 2, 4).reshape(n * G, nch, CMP_STRIDE * hd)

                def comp(r, w1, w2, pe):
                    n = r.shape[0]
                    c = _compress(chunks_of(r), w1, w2, pe)
                    return c.reshape(n, G, c.shape[1], hd).transpose(0, 2, 1, 3)

                kc_p = comp(rows_p[:, :, 0], w_cmp_k1, w_cmp_k2, pe_cmp_k)
                vc_p = comp(rows_p[:, :, 1], w_cmp_v1, w_cmp_v2, pe_cmp_v)
                assert (past + T) // CMP_STRIDE == past // CMP_STRIDE and page_table.shape[1] % CMP_P == 0
                kc_s, vc_s = _compress_cache(cache_kv, page_table, w_cmp_k1, w_cmp_k2, pe_cmp_k,
                                             w_cmp_v1, w_cmp_v2, pe_cmp_v)
                nc_p = kc_p.shape[1]
                ncp = -(-nc_p // 128) * 128
                ns_p = -(-S // SLC_LEN)

                def pad_c(c):
                    c = jnp.pad(c, ((0, 0), (0, ncp - nc_p), (0, 0), (0, 0)))
                    return c.transpose(0, 2, 1, 3).reshape(B * G, ncp, hd)

                kc_pp, vc_pp = pad_c(kc_p), pad_c(vc_p)
                ov_p = _overlap(ncp, ns_p).astype(bf16)
            li = l - N_A_LAYERS
            hq = NSA_HEADS * hd
            w_in = w_nsa_in[li]
            q = _mm(xt_b, w_in[:, :hq])
            gates = jax.nn.sigmoid(_mm(xt_b, w_in[:, hq:]))
            o_cmp, score = _cmp_prompt(q, kc_pp, vc_pp, ov_p, B, S)
            sel = _select(score.reshape(B, G, S, ns_p), pos_p, ns_p)
            selmask = jnp.sum(jax.nn.one_hot(sel, ns_p, dtype=f32), axis=-2).reshape(B * G, S, ns_p)
            gates_g = gates.reshape(Ttot, G, 3 * HPG).transpose(1, 0, 2)
            merged = _slc_win_prompt(q, pos_p, selmask, rows, win_rows, o_cmp, gates_g, B, S)
            q_s = q[P:].reshape(DB, TP, G, HPG, hd)[:, :T]
            qr_s = _rope_heads(q[P:], pos_flat[P:]).reshape(DB, TP, G, HPG, hd)[:, :T]
            oc_s, os_s, ow_s = _nsa_sample(q_s, qr_s, pos_s, rows_s, win_keys_s, kc_s, vc_s, cache_kv, page_table)
            g_s = gates[P:].reshape(DB, TP, G, HPG, 3)[:, :T]
            m_s = g_s[..., 0:1] * oc_s + g_s[..., 1:2] * os_s + g_s[..., 2:3] * ow_s
            m_s = jnp.pad(m_s.reshape(DB, T, hq), ((0, 0), (0, TP - T), (0, 0))).reshape(DB * TP, hq)
            merged = lax.dynamic_update_slice(merged, m_s.astype(merged.dtype), (P, 0))
            y = _mm(merged, w_nsa_out, layer=li)
        xt = _post_norm(xt, y, ln_gain[l, 0], ln_bias[l, 0])
        ya, yb, wk = _moe(xt, w_router, b_router, w_exp_gate, w_exp_up, w_exp_down, l)
        xt, xt_b = _post_norm_combine(xt, ya, yb, wk, ln_gain[l, 1], ln_bias[l, 1])

    xp = xt[:P].reshape(B, S, D)
    xs = xt[P:].reshape(DB, TP, D)[:, :T]
    win_rows_p = win_rows[:P].reshape(B, S, 2, G, hd)
    if S >= wb:
        win_p = win_rows_p[:, S - wb:]
    else:
        win_p = jnp.pad(win_rows_p, ((0, 0), (wb - S, 0), (0, 0), (0, 0), (0, 0)))
    win_s = win_keys_s[:, T:]
    return (xp, xs, rows_p, rows_s, win_p, win_s, jnp.stack(ret_p), jnp.stack(ret_s))
```

```python
import functools

import jax
import jax.numpy as jnp
from jax import lax
from jax.experimental import pallas as pl
from jax.experimental.pallas import tpu as pltpu

f32 = jnp.float32
bf16 = jnp.bfloat16

D_MODEL = 2048
DEPTH = 4
PAGE_SIZE = 128
N_A_LAYERS = DEPTH // 2
RET_HEADS = 8
RET_DK = D_MODEL // RET_HEADS
RET_DV = 2 * RET_DK
RET_CHUNK = 128
RET_THETA = 10000.0
NSA_HEADS = 16
NSA_HD = D_MODEL // NSA_HEADS
NSA_KV = 4
NSA_HPG = NSA_HEADS // NSA_KV
CMP_LEN = 32
CMP_STRIDE = 16
SLC_LEN = 64
N_SEL = 16
WINDOW = 512
QBLK = 128
ROPE_THETA = 500000.0
ROT_DIM = NSA_HD // 4
N_KV_PROJ = 6
N_EXPERTS = 16
N_GROUPS = 4
EXPERTS_PER_GROUP = N_EXPERTS // N_GROUPS
TOPK_GROUPS = 1
TOP_K = 2
D_EXPERT = 1408
DN_ALPHA = (2.0 * DEPTH) ** 0.25
LN_EPS = 1e-5

TP = 16
MOE_TM = 512
MOE_TF = 128
VMEM_LIMIT = 56 * 1024 * 1024
NEG = -1e30


def _cparams(sem):
    return pltpu.CompilerParams(dimension_semantics=sem, vmem_limit_bytes=VMEM_LIMIT)


def _bdot(a, b):
    return jnp.dot(a.astype(bf16), b.astype(bf16), preferred_element_type=f32)


def _bdot_nt(a, b):
    return lax.dot_general(a.astype(bf16), b.astype(bf16), (((1,), (1,)), ((), ())), preferred_element_type=f32)


def _mm_kernel(x_ref, w_ref, o_ref, wb_scr):
    @pl.when(pl.program_id(1) == 0)
    def _():
        wb_scr[...] = w_ref[...].astype(bf16)

    o_ref[...] = jnp.dot(x_ref[...].astype(bf16), wb_scr[...], preferred_element_type=f32)


def _mm3_kernel(x_ref, w_ref, o_ref, wb_scr):
    x = x_ref[...]
    w = w_ref[...]
    xh = x.astype(bf16)
    wh = w.astype(bf16)
    xl = (x - xh.astype(f32)).astype(bf16)
    wl = (w - wh.astype(f32)).astype(bf16)
    acc = jnp.dot(xh, wh, preferred_element_type=f32)
    acc += jnp.dot(xh, wl, preferred_element_type=f32)
    acc += jnp.dot(xl, wh, preferred_element_type=f32)
    o_ref[...] = acc


def _row_tile(m, pref=640):
    for t in (pref, 512, 256, 128):
        if m % t == 0:
            return t
    return min(512, m)


def _mm(x, w, layer=None, tn=1024, body=_mm_kernel):
    M, K = x.shape
    N = w.shape[-1]
    tn = min(tn, N)
    tm = _row_tile(M)
    if layer is None:
        w_spec = pl.BlockSpec((K, tn), lambda j, i: (0, j))
    else:
        w_spec = pl.BlockSpec((None, K, tn), lambda j, i: (layer, 0, j))
    return pl.pallas_call(
        body,
        grid=(pl.cdiv(N, tn), pl.cdiv(M, tm)),
        in_specs=[pl.BlockSpec((tm, K), lambda j, i: (i, 0)), w_spec],
        out_specs=pl.BlockSpec((tm, tn), lambda j, i: (i, j)),
        out_shape=jax.ShapeDtypeStruct((M, N), f32),
        scratch_shapes=[pltpu.VMEM((K, tn), bf16)],
        compiler_params=_cparams(("parallel", "arbitrary")),
        name="mm",
    )(x, w)


def _layer_norm(h, g, b):
    mu = jnp.mean(h, -1, keepdims=True)
    d = h - mu
    var = jnp.mean(d * d, -1, keepdims=True)
    return d * lax.rsqrt(var + LN_EPS) * g + b


def _pn_combine_kernel(x_ref, ya_ref, yb_ref, w_ref, g_ref, b_ref, o_ref, ob_ref):
    w = w_ref[...]
    y = ya_ref[...] * w[:, 0:1] + yb_ref[...] * w[:, 1:2]
    o = _layer_norm(DN_ALPHA * x_ref[...] + y, g_ref[...], b_ref[...])
    o_ref[...] = o
    ob_ref[...] = o.astype(bf16)


MMLN_TK = 1024


def _mm_ln_kernel(x_ref, w_ref, r_ref, g_ref, b_ref, o_ref, acc_scr, *, nk):
    k = pl.program_id(1)
    @pl.when(k == 0)
    def _():
        acc_scr[...] = jnp.zeros(acc_scr.shape, f32)

    acc_scr[...] += jnp.dot(x_ref[...].astype(bf16), w_ref[...].astype(bf16), preferred_element_type=f32)

    @pl.when(k == nk - 1)
    def _():
        o_ref[...] = _layer_norm(DN_ALPHA * r_ref[...] + acc_scr[...], g_ref[...], b_ref[...])


def _mm_post_norm(x, w, layer, res, g, b):
    M, K = x.shape
    D = w.shape[-1]
    tm = _row_tile(M)
    tk = min(MMLN_TK, K)
    nk = K // tk
    row = pl.BlockSpec((tm, D), lambda i, k: (i, 0))
    vec = pl.BlockSpec((1, D), lambda i, k: (0, 0))
    return pl.pallas_call(
        functools.partial(_mm_ln_kernel, nk=nk),
        grid=(pl.cdiv(M, tm), nk),
        in_specs=[
            pl.BlockSpec((tm, tk), lambda i, k: (i, k)),
            pl.BlockSpec((None, tk, D), lambda i, k: (layer, k, 0)),
            row, vec, vec,
        ],
        out_specs=row,
        out_shape=jax.ShapeDtypeStruct((M, D), f32),
        scratch_shapes=[pltpu.VMEM((tm, D), f32)],
        compiler_params=_cparams(("parallel", "arbitrary")),
        name="mm_post_norm",
    )(x, w, res, g.reshape(1, D), b.reshape(1, D))


def _post_norm_combine(x, ya, yb, w, g, b):
    M, D = x.shape
    tm = _row_tile(M, 128)
    row = pl.BlockSpec((tm, D), lambda i: (i, 0))
    vec = pl.BlockSpec((1, D), lambda i: (0, 0))
    return pl.pallas_call(
        _pn_combine_kernel,
        grid=(pl.cdiv(M, tm),),
        in_specs=[row, row, row, pl.BlockSpec((tm, TOP_K), lambda i: (i, 0)), vec, vec],
        out_specs=[row, row],
        out_shape=[jax.ShapeDtypeStruct((M, D), f32), jax.ShapeDtypeStruct((M, D), bf16)],
        compiler_params=_cparams(("parallel",)),
        name="post_norm_combine",
    )(x, ya, yb, w, g.reshape(1, D), b.reshape(1, D))


RET_HB = 2


def _ret_kernel(q_ref, k_ref, v_ref, g_ref, cos_ref, sin_ref, intra_ref, qd_ref, kd_ref, cd_ref, s0_ref,
                o_ref, sfin_ref, s_scr, *, nc):
    c = pl.program_id(2)

    @pl.when(c == 0)
    def _():
        s_scr[...] = s0_ref[...]

    half = RET_DK // 2
    cos = cos_ref[...]
    sin = sin_ref[...]

    def rot(t):
        t1, t2 = t[:, :half], t[:, half:]
        return jnp.concatenate([t1 * cos - t2 * sin, t2 * cos + t1 * sin], axis=-1)

    for hh in range(RET_HB):
        kcol = slice(hh * RET_DK, (hh + 1) * RET_DK)
        vcol = slice(hh * RET_DV, (hh + 1) * RET_DV)
        q = rot(q_ref[:, kcol])
        k = rot(k_ref[:, kcol]) * (RET_DK ** -0.5)
        v = v_ref[:, vcol].astype(bf16)
        s = s_scr[hh]
        att = _bdot_nt(q, k) * intra_ref[hh]
        o = _bdot(att, v) + _bdot(q * qd_ref[hh], s)
        kd = (k * kd_ref[hh]).T
        s_scr[hh] = s * cd_ref[hh] + _bdot(kd, v)
        mu = jnp.mean(o, -1, keepdims=True)
        d = o - mu
        var = jnp.mean(d * d, -1, keepdims=True)
        gate = g_ref[:, vcol]
        o_ref[:, vcol] = (gate * jax.nn.sigmoid(gate) * (d * lax.rsqrt(var + LN_EPS))).astype(o_ref.dtype)

    @pl.when(c == nc - 1)
    def _():
        sfin_ref[...] = s_scr[...]


def _ret_decay(chunk, valid):
    log_g = jnp.log(1.0 - 2.0 ** (-5.0 - jnp.arange(RET_HEADS, dtype=f32)))
    i = jnp.arange(chunk, dtype=f32)
    real = i < valid
    diff = i[:, None] - i[None, :]
    intra = jnp.where((diff >= 0) & real[None, :], jnp.exp(log_g[:, None, None] * jnp.maximum(diff, 0.0)), 0.0)
    q_dec = jnp.exp(log_g[:, None] * (i + 1.0))[:, :, None]
    k_dec = jnp.where(real, jnp.exp(log_g[:, None] * jnp.maximum(valid - 1.0 - i, 0.0)), 0.0)[:, :, None]
    c_dec = jnp.exp(log_g * valid)[:, None, None]
    return intra, q_dec, k_dec, c_dec


def _ret_tables(pos):
    inv = 1.0 / (RET_THETA ** jnp.linspace(0.0, 1.0, RET_DK // 2, dtype=f32))
    ang = pos.astype(f32)[:, None] * inv[None, :]
    return jnp.cos(ang), jnp.sin(ang)


def _retention(proj, pos, s0, nb, nc, chunk, valid, row0, o_prev):
    Ttot = proj.shape[0]
    hk, hv = RET_HEADS * RET_DK, RET_HEADS * RET_DV
    rb0 = row0 // chunk
    cos, sin = _ret_tables(pos)
    intra, q_dec, k_dec, c_dec = _ret_decay(chunk, valid)
    H = RET_HEADS

    def rowblk(b, h, c):
        return rb0 + b * nc + c

    HB = RET_HB
    wk, wv = HB * RET_DK, HB * RET_DV
    in_specs = [
        pl.BlockSpec((chunk, wk), lambda b, h, c: (rowblk(b, h, c), h)),
        pl.BlockSpec((chunk, wk), lambda b, h, c: (rowblk(b, h, c), hk // wk + h)),
        pl.BlockSpec((chunk, wv), lambda b, h, c: (rowblk(b, h, c), (2 * hk) // wv + h)),
        pl.BlockSpec((chunk, wv), lambda b, h, c: (rowblk(b, h, c), (2 * hk + hv) // wv + h)),
        pl.BlockSpec((chunk, RET_DK // 2), lambda b, h, c: (c, 0)),
        pl.BlockSpec((chunk, RET_DK // 2), lambda b, h, c: (c, 0)),
        pl.BlockSpec((HB, chunk, chunk), lambda b, h, c: (h, 0, 0)),
        pl.BlockSpec((HB, chunk, 1), lambda b, h, c: (h, 0, 0)),
        pl.BlockSpec((HB, chunk, 1), lambda b, h, c: (h, 0, 0)),
        pl.BlockSpec((HB, 1, 1), lambda b, h, c: (h, 0, 0)),
        pl.BlockSpec((None, HB, RET_DK, RET_DV), lambda b, h, c: (b, h, 0, 0)),
    ]
    args = [proj, proj, proj, proj, cos, sin, intra, q_dec, k_dec, c_dec, s0]
    aliases = {}
    if o_prev is not None:
        in_specs.append(pl.BlockSpec(memory_space=pl.ANY))
        args.append(o_prev)
        aliases = {len(args) - 1: 0}

    def body(*refs):
        if o_prev is not None:
            refs = refs[:11] + refs[12:]
        _ret_kernel(*refs, nc=nc)

    return pl.pallas_call(
        body,
        grid=(nb, H // HB, nc),
        in_specs=in_specs,
        out_specs=[
            pl.BlockSpec((chunk, wv), lambda b, h, c: (rowblk(b, h, c), h)),
            pl.BlockSpec((None, HB, RET_DK, RET_DV), lambda b, h, c: (b, h, 0, 0)),
        ],
        out_shape=[jax.ShapeDtypeStruct((Ttot, hv), bf16), jax.ShapeDtypeStruct((nb, H, RET_DK, RET_DV), f32)],
        scratch_shapes=[pltpu.VMEM((HB, RET_DK, RET_DV), f32)],
        input_output_aliases=aliases,
        compiler_params=_cparams(("parallel", "parallel", "arbitrary")),
        name="retention",
    )(*args)


def _moe_kernel(be_ref, nu_ref, src_ref, cnt_ref, x_hbm, wga_ref, wgb_ref, wua_ref, wub_ref, wda_ref, wdb_ref, o_ref,
                xg_scr, xb_scr, sem, *, nf):
    i = pl.program_id(0)
    j = pl.program_id(1)
    n_used = nu_ref[0]

    def gather_rows(blk, slot):
        def start(r, carry):
            row = src_ref[blk * MOE_TM + r]
            pltpu.make_async_copy(x_hbm.at[pl.ds(row, 1)], xg_scr.at[slot, pl.ds(r, 1)], sem.at[slot]).start()
            return carry
        lax.fori_loop(0, MOE_TM, start, 0, unroll=8)

    @pl.when((i < n_used) & (j == 0))
    def _():
        slot = i % 2

        @pl.when(i == 0)
        def _():
            gather_rows(0, 0)

        pltpu.make_async_copy(x_hbm.at[pl.ds(0, MOE_TM)], xg_scr.at[slot], sem.at[slot]).wait()

        @pl.when(i + 1 < n_used)
        def _():
            gather_rows(i + 1, 1 - slot)

        xb_scr[...] = xg_scr[slot].astype(bf16)
        o_ref[...] = jnp.zeros(o_ref.shape, f32)

    def expert_mlp(m):
        xb = xb_scr[:m]
        wg = jnp.concatenate([wga_ref[...].astype(bf16), wgb_ref[...].astype(bf16)], axis=1)
        wu = jnp.concatenate([wua_ref[...].astype(bf16), wub_ref[...].astype(bf16)], axis=1)
        wd = jnp.concatenate([wda_ref[...].astype(bf16), wdb_ref[...].astype(bf16)], axis=0)
        g = jnp.dot(xb, wg, preferred_element_type=f32)
        u = jnp.dot(xb, wu, preferred_element_type=f32)
        h = g * jax.nn.sigmoid(g) * u
        live = jnp.where(2 * j + 1 < nf, 2 * MOE_TF, MOE_TF)
        h = jnp.where(lax.broadcasted_iota(jnp.int32, h.shape, 1) < live, h, 0.0)
        o_ref[:m] += jnp.dot(h.astype(bf16), wd, preferred_element_type=f32)

    half_full = cnt_ref[jnp.minimum(i, n_used - 1)] <= MOE_TM // 2

    @pl.when((i < n_used) & jnp.logical_not(half_full))
    def _():
        expert_mlp(MOE_TM)

    @pl.when((i < n_used) & half_full)
    def _():
        expert_mlp(MOE_TM // 2)


def _moe_experts(x, rows_src, blk_e, blk_rows, n_used, w_g, w_u, w_d, layer):
    D = x.shape[1]
    R = rows_src.shape[0]
    n_blk = R // MOE_TM
    nf = D_EXPERT // MOE_TF
    nsteps = -(-nf // 2)

    def ii(i, nu):
        return jnp.minimum(i, nu[0] - 1)

    def jj(i, j, nu, odd):
        return jnp.minimum(2 * jnp.where(i < nu[0], j, nsteps - 1) + odd, nf - 1)

    def w_in_spec(odd):
        return pl.BlockSpec((None, None, D, MOE_TF),
                            lambda i, j, be, nu, src, cnt: (layer, be[ii(i, nu)], 0, jj(i, j, nu, odd)))

    def w_out_spec(odd):
        return pl.BlockSpec((None, None, MOE_TF, D),
                            lambda i, j, be, nu, src, cnt: (layer, be[ii(i, nu)], jj(i, j, nu, odd), 0))

    grid_spec = pltpu.PrefetchScalarGridSpec(
        num_scalar_prefetch=4,
        grid=(n_blk, nsteps),
        in_specs=[
            pl.BlockSpec(memory_space=pl.ANY),
            w_in_spec(0), w_in_spec(1), w_in_spec(0), w_in_spec(1), w_out_spec(0), w_out_spec(1),
        ],
        out_specs=pl.BlockSpec((MOE_TM, D), lambda i, j, be, nu, src, cnt: (ii(i, nu), 0)),
        scratch_shapes=[
            pltpu.VMEM((2, MOE_TM, D), f32),
            pltpu.VMEM((MOE_TM, D), bf16),
            pltpu.SemaphoreType.DMA((2,)),
        ],
    )
    return pl.pallas_call(
        functools.partial(_moe_kernel, nf=nf),
        grid_spec=grid_spec,
        out_shape=jax.ShapeDtypeStruct((R, D), f32),
        compiler_params=_cparams(("arbitrary", "arbitrary")),
        name="moe_experts",
    )(blk_e, n_used, rows_src, blk_rows, x, w_g, w_g, w_u, w_u, w_d, w_d)


def _topk_small(x, k):
    lane = jnp.arange(x.shape[-1])
    vals, idxs = [], []
    for _ in range(k):
        i = jnp.argmax(x, axis=-1)
        vals.append(jnp.max(x, axis=-1))
        idxs.append(i)
        x = jnp.where(lane == i[..., None], -jnp.inf, x)
    return jnp.stack(vals, -1), jnp.stack(idxs, -1)


def _moe(x, w_router, b_router, w_g, w_u, w_d, layer):
    T, D = x.shape
    logits = _mm(x, w_router, body=_mm3_kernel)
    aff = jax.nn.sigmoid(logits)
    biased = aff + b_router.astype(f32)
    grp = biased.reshape(T, N_GROUPS, EXPERTS_PER_GROUP)
    grp_score = _topk_small(grp, 2)[0].sum(-1)
    _, top_g = _topk_small(grp_score, TOPK_GROUPS)
    g_mask = jnp.any(top_g[:, :, None] == jnp.arange(N_GROUPS)[None, None, :], axis=1)
    e_mask = jnp.repeat(g_mask, EXPERTS_PER_GROUP, axis=1)
    _, idx = _topk_small(jnp.where(e_mask, biased, -jnp.inf), TOP_K)
    w = jnp.take_along_axis(aff, idx, axis=1)
    w = w / jnp.sum(w, -1, keepdims=True)
    A = T * TOP_K
    e_flat = idx.reshape(A).astype(jnp.int32)
    order = jnp.argsort(e_flat)
    hot = (e_flat[:, None] == jnp.arange(N_EXPERTS, dtype=jnp.int32)[None, :]).astype(jnp.int32)
    counts = jnp.sum(hot, axis=0)
    start = jnp.cumsum(counts) - counts
    padded = (counts + MOE_TM - 1) // MOE_TM * MOE_TM
    pend = jnp.cumsum(padded)
    pstart = pend - padded
    n_blk = -(-A // MOE_TM) + N_EXPERTS
    blk_e = jnp.minimum(jnp.searchsorted(pend, jnp.arange(n_blk) * MOE_TM, side='right'), N_EXPERTS - 1)
    blk_e = blk_e.astype(jnp.int32)
    n_used = (pend[-1] // MOE_TM).astype(jnp.int32).reshape(1)
    blk_rows = jnp.clip(counts[blk_e] - (jnp.arange(n_blk) * MOE_TM - pstart[blk_e]), 0, MOE_TM).astype(jnp.int32)
    slot = jnp.arange(n_blk * MOE_TM, dtype=jnp.int32)
    e_slot = blk_e[slot // MOE_TM]
    rank = slot - pstart[e_slot]
    tok = order[jnp.clip(start[e_slot] + rank, 0, A - 1)] // TOP_K
    rows_src = jnp.where(rank < counts[e_slot], tok, slot % T).astype(jnp.int32)
    out = _moe_experts(x, rows_src, blk_e, blk_rows, n_used, w_g, w_u, w_d, layer)
    rank_a = jnp.sum(jnp.cumsum(hot, axis=0) * hot, axis=1) - 1
    pos = (pstart[e_flat] + rank_a).astype(jnp.int32).reshape(T, TOP_K)
    return out[pos[:, 0]], out[pos[:, 1]], w


def _nsa_inv_freq():
    return ROPE_THETA ** (-jnp.arange(0, ROT_DIM, 2, dtype=f32) / ROT_DIM)


def _rope_heads(x, pos):
    T = x.shape[0]
    half = ROT_DIM // 2
    ang = pos.astype(f32)[:, None] * _nsa_inv_freq()[None, :]
    cos = jnp.cos(ang)[:, None, :]
    sin = jnp.sin(ang)[:, None, :]
    xh = x.reshape(T, -1, NSA_HD)
    x1, x2 = xh[..., :half], xh[..., half:ROT_DIM]
    out = jnp.concatenate([x1 * cos - x2 * sin, x2 * cos + x1 * sin, xh[..., ROT_DIM:]], axis=-1)
    return out.reshape(x.shape)


def _masked_softmax(s, mask):
    s = jnp.where(mask, s, NEG)
    m = jnp.max(s, axis=-1, keepdims=True)
    m = jnp.where(m <= 0.5 * NEG, 0.0, m)
    e = jnp.where(mask, jnp.exp(s - m), 0.0)
    den = jnp.sum(e, axis=-1, keepdims=True)
    return e / jnp.where(den > 0, den, 1.0)


def _compress(chunks, w1, w2, pe):
    n, nchunk, kk = chunks.shape
    wab = jnp.concatenate([w1[:kk], w1[kk:]], axis=1)
    y = _mm(chunks.reshape(n * nchunk, kk), wab).reshape(n, nchunk, 2 * NSA_HD)
    bias = _mm(jnp.broadcast_to(pe.reshape(1, CMP_LEN * NSA_HD), (8, CMP_LEN * NSA_HD)), w1)[:1]
    pre = y[:, :-1, :NSA_HD] + y[:, 1:, NSA_HD:] + bias
    nc = nchunk - 1
    return _mm(jax.nn.silu(pre).reshape(n * nc, NSA_HD), w2).reshape(n, nc, NSA_HD)


CMP_P = 8


def _cmp_pages_kernel(pt_ref, *refs):
    P = CMP_P
    page_refs, w_ref, o_ref, x_scr = refs[:2 * P], refs[2 * P], refs[2 * P + 1], refs[2 * P + 2]
    cpp = PAGE_SIZE // CMP_STRIDE
    for typ in range(2):
        for r in range(CMP_STRIDE):
            for g in range(NSA_KV):
                for p in range(P):
                    row = (g * P + p) * cpp
                    x_scr[row:row + cpp, r * NSA_HD:(r + 1) * NSA_HD] = (
                        page_refs[typ * P + p][pl.ds(r, cpp, stride=CMP_STRIDE), g, :])
        y = _bdot(x_scr[...], w_ref[typ])
        for g in range(NSA_KV):
            o_ref[typ, g] = y[g * P * cpp:(g + 1) * P * cpp]


def _cmp_pages(cache_kv, page_table, wab):
    DB, n_pages = page_table.shape
    P = CMP_P
    G, d = NSA_KV, NSA_HD
    cpp = PAGE_SIZE // CMP_STRIDE
    kk = CMP_STRIDE * d

    def page_map(p, typ):
        return lambda b, j, pt: (pt[b * n_pages + j * P + p], 0, typ, 0, 0)

    in_specs = [pl.BlockSpec((None, PAGE_SIZE, None, G, d), page_map(p, typ)) for typ in range(2) for p in range(P)]
    in_specs.append(pl.BlockSpec((2, kk, 2 * d), lambda b, j, pt: (0, 0, 0)))
    return pl.pallas_call(
        _cmp_pages_kernel,
        grid_spec=pltpu.PrefetchScalarGridSpec(
            num_scalar_prefetch=1,
            grid=(DB, n_pages // P),
            in_specs=in_specs,
            out_specs=pl.BlockSpec((None, 2, G, P * cpp, 2 * d), lambda b, j, pt: (b, 0, 0, j, 0)),
            scratch_shapes=[pltpu.VMEM((G * P * cpp, kk), f32)],
        ),
        out_shape=jax.ShapeDtypeStruct((DB, 2, G, n_pages * cpp, 2 * d), f32),
        compiler_params=_cparams(("arbitrary", "arbitrary")),
        name="cmp_pages",
    )(page_table.reshape(-1).astype(jnp.int32), *([cache_kv] * (2 * P)), wab)


def _compress_cache(cache_kv, page_table, w_k1, w_k2, pe_k, w_v1, w_v2, pe_v):
    DB = page_table.shape[0]
    G, d = NSA_KV, NSA_HD
    kk = CMP_STRIDE * d
    wab = jnp.stack([jnp.concatenate([w[:kk], w[kk:]], axis=1) for w in (w_k1, w_v1)])
    y = _cmp_pages(cache_kv, page_table, wab)
    out = []
    for typ, (w1, w2, pe) in enumerate(((w_k1, w_k2, pe_k), (w_v1, w_v2, pe_v))):
        bias = _mm(jnp.broadcast_to(pe.reshape(1, CMP_LEN * d), (8, CMP_LEN * d)), w1)[:1]
        pre = y[:, typ, :, :-1, :d] + y[:, typ, :, 1:, d:] + bias
        nc = pre.shape[2]
        c = _mm(jax.nn.silu(pre).reshape(DB * G * nc, d), w2)
        out.append(c.reshape(DB, G, nc, d).transpose(0, 2, 1, 3))
    return out


def _cmp_kernel(q_ref, kc_ref, vc_ref, ov_ref, o_ref, sc_ref):
    qi = pl.program_id(2)
    ncp = kc_ref.shape[0]
    H, d = NSA_HPG, NSA_HD
    rows = lax.broadcasted_iota(jnp.int32, (H * QBLK, ncp), 0)
    qpos = qi * QBLK + rows % QBLK
    cend = lax.broadcasted_iota(jnp.int32, (H * QBLK, ncp), 1) * CMP_STRIDE + (CMP_LEN - 1)
    mask = cend <= qpos
    kc = kc_ref[...].astype(bf16)
    vc = vc_ref[...].astype(bf16)
    q4 = jnp.concatenate([q_ref[:, h * d:(h + 1) * d].astype(bf16) for h in range(H)], axis=0)
    s = _bdot_nt(q4, kc) * (d ** -0.5)
    p = _masked_softmax(s, mask).astype(bf16)
    o4 = jnp.dot(p, vc, preferred_element_type=f32)
    sc4 = jnp.dot(p, ov_ref[...], preferred_element_type=f32)
    for h in range(H):
        o_ref[:, h * d:(h + 1) * d] = o4[h * QBLK:(h + 1) * QBLK]
    sc_ref[...] = functools.reduce(jnp.add, [sc4[h * QBLK:(h + 1) * QBLK] for h in range(H)])


def _cmp_prompt(q, kc, vc, overlap, B, S):
    Ttot = q.shape[0]
    nq = S // QBLK
    ncp = kc.shape[1]
    ns = overlap.shape[1]
    G = NSA_KV
    gw = NSA_HPG * NSA_HD
    return pl.pallas_call(
        _cmp_kernel,
        grid=(B, G, nq),
        in_specs=[
            pl.BlockSpec((QBLK, gw), lambda b, g, i: (b * nq + i, g)),
            pl.BlockSpec((None, ncp, NSA_HD), lambda b, g, i: (b * G + g, 0, 0)),
            pl.BlockSpec((None, ncp, NSA_HD), lambda b, g, i: (b * G + g, 0, 0)),
            pl.BlockSpec((ncp, ns), lambda b, g, i: (0, 0)),
        ],
        out_specs=[
            pl.BlockSpec((QBLK, gw), lambda b, g, i: (b * nq + i, g)),
            pl.BlockSpec((None, QBLK, ns), lambda b, g, i: (b * G + g, i, 0)),
        ],
        out_shape=[jax.ShapeDtypeStruct((Ttot, G * gw), f32), jax.ShapeDtypeStruct((B * G, S, ns), f32)],
        compiler_params=_cparams(("parallel", "parallel", "parallel")),
        name="cmp_prompt",
    )(q, kc, vc, overlap)


SEL_TK = 512


LOG2E = 1.4426950408889634
LANES = 128


def _fold_lanes(x, op):
    parts = [x[:, j * LANES:(j + 1) * LANES] for j in range(x.shape[1] // LANES)]
    return functools.reduce(op, parts)


def _slc_win_kernel(q_ref, rc_ref, rs1_ref, rs2_ref, sel_ref, ks_ref, vs_ref, kw_ref, vw_ref, oc_ref, gt_ref, o_ref,
                    s_scr, m_scr, l_scr, acc_scr):
    qi = pl.program_id(2)
    H, d = NSA_HPG, NSA_HD
    c2 = (d ** -0.5) * LOG2E
    rc, rs1, rs2 = rc_ref[...], rs1_ref[...], rs2_ref[...]
    half = ROT_DIM // 2

    def roped(h):
        x = q_ref[:, h * d:(h + 1) * d]
        return x * rc + pltpu.roll(x, d - half, 1) * rs1 + pltpu.roll(x, half, 1) * rs2

    q4 = jnp.concatenate([(roped(h) * c2).astype(bf16) for h in range(H)], axis=0)
    sel = sel_ref[...].astype(bf16)
    ns = sel.shape[1]
    bpt = SEL_TK // SLC_LEN
    nl = SEL_TK // LANES

    m_scr[...] = jnp.full(m_scr.shape, NEG, f32)
    l_scr[...] = jnp.zeros(l_scr.shape, f32)
    acc_scr[...] = jnp.zeros(acc_scr.shape, f32)
    qpos = qi * QBLK + lax.broadcasted_iota(jnp.int32, (QBLK, SEL_TK), 0)
    n_tiles = (qi * QBLK + QBLK - 1) // SEL_TK + 1

    def scores(t, carry):
        k0 = pl.multiple_of(t * SEL_TK, SEL_TK)
        k = ks_ref[pl.ds(k0, SEL_TK), :].astype(bf16)
        blk = lax.broadcasted_iota(jnp.int32, (ns, SEL_TK), 1) // SLC_LEN + t * bpt
        expand = jnp.where(blk == lax.broadcasted_iota(jnp.int32, (ns, SEL_TK), 0), 1.0, 0.0).astype(bf16)
        chosen = jnp.dot(sel, expand, preferred_element_type=f32) > 0.5
        kpos = k0 + lax.broadcasted_iota(jnp.int32, (QBLK, SEL_TK), 1)
        bias = jnp.where(chosen & (kpos <= qpos), 0.0, NEG)
        s4 = _bdot_nt(q4, k)
        for h in range(H):
            rows = slice(h * QBLK, (h + 1) * QBLK)
            s = s4[rows] + bias
            s_scr[t, rows] = s
            m_scr[rows] = jnp.maximum(m_scr[rows], _fold_lanes(s, jnp.maximum))
        return carry

    lax.fori_loop(0, n_tiles, scores, 0)
    m_scr[...] = jnp.broadcast_to(jnp.max(m_scr[...], axis=-1, keepdims=True), m_scr.shape)

    def accumulate(t, carry):
        k0 = pl.multiple_of(t * SEL_TK, SEL_TK)
        v = vs_ref[pl.ds(k0, SEL_TK), :].astype(bf16)
        s = s_scr[t]
        m = m_scr[...]
        p = jnp.concatenate([jnp.exp2(s[:, j * LANES:(j + 1) * LANES] - m) for j in range(nl)], axis=-1)
        l_scr[...] += _fold_lanes(p, jnp.add)
        acc_scr[...] += jnp.dot(p.astype(bf16), v, preferred_element_type=f32)
        return carry

    lax.fori_loop(0, n_tiles, accumulate, 0)
    o_slc = acc_scr[...] / jnp.sum(l_scr[...], axis=-1, keepdims=True)

    wk = WINDOW + QBLK
    w0 = pl.multiple_of(jnp.maximum(qi * QBLK - WINDOW, 0), QBLK)
    kw = kw_ref[pl.ds(w0, wk), :].astype(bf16)
    vw = vw_ref[pl.ds(w0, wk), :].astype(bf16)
    qp = qi * QBLK + lax.broadcasted_iota(jnp.int32, (QBLK, wk), 0)
    kp = w0 + lax.broadcasted_iota(jnp.int32, (QBLK, wk), 1)
    wbias = jnp.where((kp <= qp) & (kp >= qp - WINDOW), 0.0, NEG)
    sw = _bdot_nt(q4, kw)
    sw = jnp.concatenate([sw[h * QBLK:(h + 1) * QBLK] + wbias for h in range(H)], axis=0)
    pw = jnp.exp2(sw - jnp.max(sw, axis=-1, keepdims=True))
    o_win = jnp.dot(pw.astype(bf16), vw, preferred_element_type=f32) / jnp.sum(pw, axis=-1, keepdims=True)

    gt = gt_ref[...]
    for h in range(H):
        rows = slice(h * QBLK, (h + 1) * QBLK)
        o = (gt[:, 3 * h:3 * h + 1] * oc_ref[:, h * d:(h + 1) * d]
             + gt[:, 3 * h + 1:3 * h + 2] * o_slc[rows] + gt[:, 3 * h + 2:3 * h + 3] * o_win[rows])
        o_ref[:, h * d:(h + 1) * d] = o.astype(o_ref.dtype)


def _rope_tables(pos):
    half = ROT_DIM // 2
    ang = pos.astype(f32)[:, None] * _nsa_inv_freq()[None, :]
    cos, sin = jnp.cos(ang), jnp.sin(ang)
    n = pos.shape[0]
    c = jnp.concatenate([cos, cos, jnp.ones((n, NSA_HD - ROT_DIM), f32)], axis=1)
    s1 = jnp.concatenate([-sin, jnp.zeros((n, NSA_HD - half), f32)], axis=1)
    s2 = jnp.concatenate([jnp.zeros((n, half), f32), sin, jnp.zeros((n, NSA_HD - ROT_DIM), f32)], axis=1)
    return c, s1, s2


def _slc_win_prompt(q, pos, selmask, rows, win_rows, o_cmp, gates, B, S):
    Ttot = q.shape[0]
    nq = S // QBLK
    ns = selmask.shape[-1]
    G = NSA_KV
    gw = NSA_HPG * NSA_HD
    rc, rs1, rs2 = _rope_tables(pos)
    qspec = pl.BlockSpec((QBLK, gw), lambda b, g, i: (b * nq + i, g))
    tspec = pl.BlockSpec((QBLK, NSA_HD), lambda b, g, i: (i, 0))
    return pl.pallas_call(
        _slc_win_kernel,
        grid=(B, G, nq),
        in_specs=[
            qspec, tspec, tspec, tspec,
            pl.BlockSpec((None, QBLK, ns), lambda b, g, i: (b * G + g, i, 0)),
            pl.BlockSpec((S, NSA_HD), lambda b, g, i: (b, 2 * G + g)),
            pl.BlockSpec((S, NSA_HD), lambda b, g, i: (b, 3 * G + g)),
            pl.BlockSpec((S, NSA_HD), lambda b, g, i: (b, g)),
            pl.BlockSpec((S, NSA_HD), lambda b, g, i: (b, G + g)),
            qspec,
            pl.BlockSpec((None, QBLK, 3 * NSA_HPG), lambda b, g, i: (g, b * nq + i, 0)),
        ],
        out_specs=qspec,
        out_shape=jax.ShapeDtypeStruct((Ttot, G * gw), bf16),
        scratch_shapes=[
            pltpu.VMEM((S // SEL_TK, NSA_HPG * QBLK, SEL_TK), f32),
            pltpu.VMEM((NSA_HPG * QBLK, LANES), f32),
            pltpu.VMEM((NSA_HPG * QBLK, LANES), f32),
            pltpu.VMEM((NSA_HPG * QBLK, NSA_HD), f32),
        ],
        compiler_params=_cparams(("parallel", "parallel", "arbitrary")),
        name="slc_win_prompt",
    )(q, rc, rs1, rs2, selmask, rows, rows, win_rows, win_rows, o_cmp, gates)


def _overlap(nc, ns):
    c0 = jnp.arange(nc) * CMP_STRIDE
    j0 = jnp.arange(ns) * SLC_LEN
    return ((c0[:, None] < j0[None, :] + SLC_LEN) & (c0[:, None] + CMP_LEN > j0[None, :]))


def _select(score, qpos, ns):
    tblk = qpos // SLC_LEN
    j = jnp.arange(ns)
    forced = (j[None, :] == 0) | (j[None, :] == tblk[:, None]) | (j[None, :] == tblk[:, None] - 1)
    score = jnp.where(forced, jnp.inf, jnp.where(j[None, :] > tblk[:, None], -jnp.inf, score))
    _, idx = lax.top_k(score, min(N_SEL, ns))
    return idx


def _cmp_attention_s(q, kc, vc, qpos):
    nc = kc.shape[1]
    cend = jnp.arange(nc) * CMP_STRIDE + (CMP_LEN - 1)
    mask = cend[None, :] <= qpos[:, None]
    s = jnp.einsum('bqghd,bcgd->bghqc', q, kc, preferred_element_type=f32) * (NSA_HD ** -0.5)
    p = _masked_softmax(s, mask)
    o = jnp.einsum('bghqc,bcgd->bqghd', p, vc)
    return o, p


def _attend_s(q, k, v, mask):
    s = jnp.einsum('bqghd,bkgd->bghqk', q, k, preferred_element_type=f32) * (NSA_HD ** -0.5)
    p = _masked_softmax(s, mask[None, None, None, :, :])
    return jnp.einsum('bghqk,bkgd->bqghd', p, v)


HPAD = 16


def _slc_s_kernel(sel_ref, pt_ref, tsel_ref, q_ref, tk_ref, tv_ref, *refs, n_sel, npb, nt, past, T):
    k_refs, v_refs, o_ref = refs[:n_sel], refs[n_sel:2 * n_sel], refs[2 * n_sel]
    k_scr, v_scr = refs[2 * n_sel + 1], refs[2 * n_sel + 2]
    b, g, t = pl.program_id(0), pl.program_id(1), pl.program_id(2)
    flat = (b * NSA_KV + g) * T + t
    qpos = past + t
    L = SLC_LEN
    nkeys = k_scr.shape[0]
    part = lax.broadcasted_iota(jnp.int32, (HPAD, nkeys), 1) // L
    lim = jnp.full((HPAD, nkeys), -1, jnp.int32)
    for n in range(n_sel):
        bid = sel_ref[flat * n_sel + n]
        k_scr[n * L:(n + 1) * L, :] = k_refs[n][:, g, :]
        v_scr[n * L:(n + 1) * L, :] = v_refs[n][:, g, :]
        lim = jnp.where(part == n, jnp.where(bid < npb, qpos - bid * L, -1), lim)
    for j in range(nt):
        n = n_sel + j
        k_scr[n * L:(n + 1) * L, :] = tk_ref[j * L:(j + 1) * L, g, :]
        v_scr[n * L:(n + 1) * L, :] = tv_ref[j * L:(j + 1) * L, g, :]
        lim = jnp.where(part == n, jnp.where(tsel_ref[flat * nt + j] > 0, qpos - (npb + j) * L, -1), lim)
    used = (n_sel + nt) * L
    if used < nkeys:
        k_scr[used:, :] = jnp.zeros((nkeys - used, NSA_HD), f32)
        v_scr[used:, :] = jnp.zeros((nkeys - used, NSA_HD), f32)
    off = lax.broadcasted_iota(jnp.int32, (HPAD, nkeys), 1) % L
    s = _bdot_nt(q_ref[...], k_scr[...]) * (NSA_HD ** -0.5)
    p = _masked_softmax(s, off <= lim)
    o_ref[...] = _bdot(p, v_scr[...])


def _slc_sample(qr, sel, tail, cache_kv, page_table, past):
    DB, T, G, H, d = qr.shape
    n_sel = sel.shape[-1]
    sub = PAGE_SIZE // SLC_LEN
    npb = past // SLC_LEN
    nt = tail.shape[1] // SLC_LEN
    n_pages = page_table.shape[1]
    nkeys = -(-((n_sel + nt) * SLC_LEN) // LANES) * LANES
    q8 = jnp.pad(qr.transpose(0, 2, 1, 3, 4), ((0, 0), (0, 0), (0, 0), (0, HPAD - H), (0, 0)))
    q8 = q8.reshape(DB * G * T, HPAD, d)
    tsel = jnp.any(sel[..., None] == npb + jnp.arange(nt), axis=-2).astype(jnp.int32)

    def cache_map(n, typ):
        def f(b, g, t, sel_r, pt_r, ts_r):
            bid = sel_r[((b * G + g) * T + t) * n_sel + n]
            jp = jnp.clip(bid, 0, npb - 1)
            return (pt_r[b * n_pages + jp // sub], jp % sub, typ, 0, 0)
        return f

    in_specs = [
        pl.BlockSpec((None, HPAD, d), lambda b, g, t, *_: ((b * G + g) * T + t, 0, 0)),
        pl.BlockSpec((None, nt * SLC_LEN, None, G, d), lambda b, g, t, *_: (b, 0, 2, 0, 0)),
        pl.BlockSpec((None, nt * SLC_LEN, None, G, d), lambda b, g, t, *_: (b, 0, 3, 0, 0)),
    ]
    in_specs += [pl.BlockSpec((None, SLC_LEN, None, G, d), cache_map(n, 2)) for n in range(n_sel)]
    in_specs += [pl.BlockSpec((None, SLC_LEN, None, G, d), cache_map(n, 3)) for n in range(n_sel)]
    out = pl.pallas_call(
        functools.partial(_slc_s_kernel, n_sel=n_sel, npb=npb, nt=nt, past=past, T=T),
        grid_spec=pltpu.PrefetchScalarGridSpec(
            num_scalar_prefetch=3,
            grid=(DB, G, T),
            in_specs=in_specs,
            out_specs=pl.BlockSpec((None, HPAD, d), lambda b, g, t, *_: ((b * G + g) * T + t, 0, 0)),
            scratch_shapes=[pltpu.VMEM((nkeys, d), f32), pltpu.VMEM((nkeys, d), f32)],
        ),
        out_shape=jax.ShapeDtypeStruct((DB * G * T, HPAD, d), f32),
        compiler_params=_cparams(("arbitrary", "arbitrary", "arbitrary")),
        name="slc_sample",
    )(sel.reshape(-1), page_table.reshape(-1).astype(jnp.int32), tsel.reshape(-1), q8, tail, tail,
      *([cache_kv] * (2 * n_sel)))
    return out.reshape(DB, G, T, HPAD, d)[:, :, :, :H].transpose(0, 2, 1, 3, 4)


def _nsa_sample(q, qr, pos, rows, win_keys, kc, vc, cache_kv, page_table):
    DB, T = q.shape[:2]
    past = page_table.shape[1] * PAGE_SIZE
    sub = PAGE_SIZE // SLC_LEN
    npb = past // SLC_LEN
    ns = -(-(past + T) // SLC_LEN)
    nt = ns - npb
    tail = jnp.pad(rows, ((0, 0), (0, nt * SLC_LEN - T), (0, 0), (0, 0), (0, 0)))
    o_cmp, p_cmp = _cmp_attention_s(q, kc, vc, pos)
    nc = p_cmp.shape[-1]
    score = jnp.einsum('bghqc,cj->bgqj', p_cmp, _overlap(nc, ns).astype(f32))
    sel = _select(score, pos, ns).astype(jnp.int32)
    o_slc = _slc_sample(qr, sel, tail, cache_kv, page_table, past)
    wb = win_keys.shape[1] - T
    kpos_w = past - wb + jnp.arange(wb + T)
    wmask = (kpos_w[None, :] <= pos[:, None]) & (kpos_w[None, :] >= pos[:, None] - WINDOW)
    o_win = _attend_s(qr, win_keys[:, :, 0], win_keys[:, :, 1], wmask)
    return o_cmp, o_slc, o_win


def kernel(x_prompt, x_sample, cache_kv, cache_win, state_ret, page_table,
           w_ret_in, w_ret_out, w_kv, w_cmp_k1, w_cmp_k2, pe_cmp_k, w_cmp_v1, w_cmp_v2, pe_cmp_v,
           w_nsa_in, w_nsa_out, w_router, b_router, w_exp_gate, w_exp_up, w_exp_down, ln_gain, ln_bias):
    B, S, D = x_prompt.shape
    DB, T, _ = x_sample.shape
    past = page_table.shape[1] * PAGE_SIZE
    wb = cache_win.shape[1]
    G, HPG, hd = NSA_KV, NSA_HPG, NSA_HD
    P = B * S
    Ttot = P + DB * TP
    pos_p = jnp.arange(S, dtype=jnp.int32)
    pos_sp = past + jnp.arange(TP, dtype=jnp.int32)
    pos_s = pos_sp[:T]
    pos_flat = jnp.concatenate([jnp.tile(pos_p, B), jnp.tile(pos_sp, DB)])
    chunk = min(RET_CHUNK, S)

    xs_pad = jnp.pad(x_sample, ((0, 0), (0, TP - T), (0, 0)))
    xt = jnp.concatenate([x_prompt.reshape(P, D), xs_pad.reshape(DB * TP, D)], axis=0)
    xt_b = xt.astype(bf16)

    ret_p, ret_s = [], []
    for l in range(DEPTH):
        if l < N_A_LAYERS:
            proj = _mm(xt_b, w_ret_in, layer=l)
            s0 = jnp.zeros((B, RET_HEADS, RET_DK, RET_DV), f32)
            o, sp = _retention(proj, pos_p[:chunk * (S // chunk)], s0, B, S // chunk, chunk, float(chunk), 0, None)
            o, ss = _retention(proj, pos_sp, state_ret[l].astype(f32), DB, 1, TP, float(T), P, o)
            ret_p.append(sp.astype(state_ret.dtype))
            ret_s.append(ss.astype(state_ret.dtype))
            xt = _mm_post_norm(o, w_ret_out, l, xt, ln_gain[l, 0], ln_bias[l, 0])
        else:
            if l == N_A_LAYERS:
                kv = _mm(xt_b, w_kv)
                gd = G * hd
                rows = jnp.concatenate([kv[:, :2 * gd], _rope_heads(kv[:, 2 * gd:3 * gd], pos_flat),
                                        kv[:, 3 * gd:4 * gd]], axis=1)
                win_rows = jnp.concatenate([_rope_heads(kv[:, 4 * gd:5 * gd], pos_flat), kv[:, 5 * gd:]], axis=1)
                rows_p = rows[:P].reshape(B, S, 4, G, hd)
                rows_s = rows[P:].reshape(DB, TP, 4, G, hd)[:, :T]
                win_rows_s = win_rows[P:].reshape(DB, TP, 2, G, hd)[:, :T]
                win_keys_s = jnp.concatenate([cache_win, win_rows_s.astype(cache_win.dtype)], axis=1)

                def chunks_of(r):
                    n, N = r.shape[:2]
                    nch = N // CMP_STRIDE
                    c = r[:, :nch * CMP_STRIDE].reshape(n, nch, CMP_STRIDE, G, hd)
                    return c.transpose(0, 3, 1, 2, 4).reshape(n * G, nch, CMP_STRIDE * hd)

                def comp(r, w1, w2, pe):
                    n = r.shape[0]
                    c = _compress(chunks_of(r), w1, w2, pe)
                    return c.reshape(n, G, c.shape[1], hd).transpose(0, 2, 1, 3)

                kc_p = comp(rows_p[:, :, 0], w_cmp_k1, w_cmp_k2, pe_cmp_k)
                vc_p = comp(rows_p[:, :, 1], w_cmp_v1, w_cmp_v2, pe_cmp_v)
                assert (past + T) // CMP_STRIDE == past // CMP_STRIDE and page_table.shape[1] % CMP_P == 0
                kc_s, vc_s = _compress_cache(cache_kv, page_table, w_cmp_k1, w_cmp_k2, pe_cmp_k,
                                             w_cmp_v1, w_cmp_v2, pe_cmp_v)
                nc_p = kc_p.shape[1]
                ncp = -(-nc_p // 128) * 128
                ns_p = -(-S // SLC_LEN)

                def pad_c(c):
                    c = jnp.pad(c, ((0, 0), (0, ncp - nc_p), (0, 0), (0, 0)))
                    return c.transpose(0, 2, 1, 3).reshape(B * G, ncp, hd)

                kc_pp, vc_pp = pad_c(kc_p), pad_c(vc_p)
                ov_p = _overlap(ncp, ns_p).astype(bf16)
            li = l - N_A_LAYERS
            hq = NSA_HEADS * hd
            w_in = w_nsa_in[li]
            q = _mm(xt_b, w_in[:, :hq])
            gates = jax.nn.sigmoid(_mm(xt_b, w_in[:, hq:]))
            o_cmp, score = _cmp_prompt(q, kc_pp, vc_pp, ov_p, B, S)
            sel = _select(score.reshape(B, G, S, ns_p), pos_p, ns_p)
            selmask = jnp.sum(jax.nn.one_hot(sel, ns_p, dtype=f32), axis=-2).reshape(B * G, S, ns_p)
            gates_g = gates.reshape(Ttot, G, 3 * HPG).transpose(1, 0, 2)
            merged = _slc_win_prompt(q, pos_p, selmask, rows, win_rows, o_cmp, gates_g, B, S)
            q_s = q[P:].reshape(DB, TP, G, HPG, hd)[:, :T]
            qr_s = _rope_heads(q[P:], pos_flat[P:]).reshape(DB, TP, G, HPG, hd)[:, :T]
            oc_s, os_s, ow_s = _nsa_sample(q_s, qr_s, pos_s, rows_s, win_keys_s, kc_s, vc_s, cache_kv, page_table)
            g_s = gates[P:].reshape(DB, TP, G, HPG, 3)[:, :T]
            m_s = g_s[..., 0:1] * oc_s + g_s[..., 1:2] * os_s + g_s[..., 2:3] * ow_s
            m_s = jnp.pad(m_s.reshape(DB, T, hq), ((0, 0), (0, TP - T), (0, 0))).reshape(DB * TP, hq)
            merged = lax.dynamic_update_slice(merged, m_s.astype(merged.dtype), (P, 0))
            xt = _mm_post_norm(merged, w_nsa_out, li, xt, ln_gain[l, 0], ln_bias[l, 0])
        ya, yb, wk = _moe(xt, w_router, b_router, w_exp_gate, w_exp_up, w_exp_down, l)
        xt, xt_b = _post_norm_combine(xt, ya, yb, wk, ln_gain[l, 1], ln_bias[l, 1])

    xp = xt[:P].reshape(B, S, D)
    xs = xt[P:].reshape(DB, TP, D)[:, :T]
    win_rows_p = win_rows[:P].reshape(B, S, 2, G, hd)
    if S >= wb:
        win_p = win_rows_p[:, S - wb:]
    else:
        win_p = jnp.pad(win_rows_p, ((0, 0), (wb - S, 0), (0, 0), (0, 0), (0, 0)))
    win_s = win_keys_s[:, T:]
    return (xp, xs, rows_p, rows_s, win_p, win_s, jnp.stack(ret_p), jnp.stack(ret_s))
```

```python
import functools

import jax
import jax.numpy as jnp
from jax import lax
from jax.experimental import pallas as pl
from jax.experimental.pallas import tpu as pltpu

f32 = jnp.float32
bf16 = jnp.bfloat16

D_MODEL = 2048
DEPTH = 4
PAGE_SIZE = 128
N_A_LAYERS = DEPTH // 2
RET_HEADS = 8
RET_DK = D_MODEL // RET_HEADS
RET_DV = 2 * RET_DK
RET_CHUNK = 128
RET_THETA = 10000.0
NSA_HEADS = 16
NSA_HD = D_MODEL // NSA_HEADS
NSA_KV = 4
NSA_HPG = NSA_HEADS // NSA_KV
CMP_LEN = 32
CMP_STRIDE = 16
SLC_LEN = 64
N_SEL = 16
WINDOW = 512
QBLK = 128
ROPE_THETA = 500000.0
ROT_DIM = NSA_HD // 4
N_KV_PROJ = 6
N_EXPERTS = 16
N_GROUPS = 4
EXPERTS_PER_GROUP = N_EXPERTS // N_GROUPS
TOPK_GROUPS = 1
TOP_K = 2
D_EXPERT = 1408
DN_ALPHA = (2.0 * DEPTH) ** 0.25
LN_EPS = 1e-5

TP = 16
MOE_TM = 512
MOE_TF = 128
MOE_NT = 4
VMEM_LIMIT = 56 * 1024 * 1024
NEG = -1e30


def _cparams(sem):
    return pltpu.CompilerParams(dimension_semantics=sem, vmem_limit_bytes=VMEM_LIMIT)


def _bdot(a, b):
    return jnp.dot(a.astype(bf16), b.astype(bf16), preferred_element_type=f32)


def _bdot_nt(a, b):
    return lax.dot_general(a.astype(bf16), b.astype(bf16), (((1,), (1,)), ((), ())), preferred_element_type=f32)


def _mm_kernel(x_ref, w_ref, o_ref, wb_scr):
    @pl.when(pl.program_id(1) == 0)
    def _():
        wb_scr[...] = w_ref[...].astype(bf16)

    o_ref[...] = jnp.dot(x_ref[...].astype(bf16), wb_scr[...], preferred_element_type=f32)


def _mm3_kernel(x_ref, w_ref, o_ref, wb_scr):
    x = x_ref[...]
    w = w_ref[...]
    xh = x.astype(bf16)
    wh = w.astype(bf16)
    xl = (x - xh.astype(f32)).astype(bf16)
    wl = (w - wh.astype(f32)).astype(bf16)
    acc = jnp.dot(xh, wh, preferred_element_type=f32)
    acc += jnp.dot(xh, wl, preferred_element_type=f32)
    acc += jnp.dot(xl, wh, preferred_element_type=f32)
    o_ref[...] = acc


def _row_tile(m, pref=640):
    for t in (pref, 512, 256, 128):
        if m % t == 0:
            return t
    return min(512, m)


def _mm(x, w, layer=None, tn=1024, body=_mm_kernel):
    M, K = x.shape
    N = w.shape[-1]
    tn = min(tn, N)
    tm = _row_tile(M)
    if layer is None:
        w_spec = pl.BlockSpec((K, tn), lambda j, i: (0, j))
    else:
        w_spec = pl.BlockSpec((None, K, tn), lambda j, i: (layer, 0, j))
    return pl.pallas_call(
        body,
        grid=(pl.cdiv(N, tn), pl.cdiv(M, tm)),
        in_specs=[pl.BlockSpec((tm, K), lambda j, i: (i, 0)), w_spec],
        out_specs=pl.BlockSpec((tm, tn), lambda j, i: (i, j)),
        out_shape=jax.ShapeDtypeStruct((M, N), f32),
        scratch_shapes=[pltpu.VMEM((K, tn), bf16)],
        compiler_params=_cparams(("parallel", "arbitrary")),
        name="mm",
    )(x, w)


def _layer_norm(h, g, b):
    mu = jnp.mean(h, -1, keepdims=True)
    d = h - mu
    var = jnp.mean(d * d, -1, keepdims=True)
    return d * lax.rsqrt(var + LN_EPS) * g + b


def _pn_combine_kernel(x_ref, ya_ref, yb_ref, w_ref, g_ref, b_ref, o_ref, ob_ref):
    w = w_ref[...]
    y = ya_ref[...] * w[:, 0:1] + yb_ref[...] * w[:, 1:2]
    o = _layer_norm(DN_ALPHA * x_ref[...] + y, g_ref[...], b_ref[...])
    o_ref[...] = o
    ob_ref[...] = o.astype(bf16)


MMLN_TK = 1024


def _mm_ln_kernel(x_ref, w_ref, r_ref, g_ref, b_ref, o_ref, acc_scr, *, nk):
    k = pl.program_id(1)
    @pl.when(k == 0)
    def _():
        acc_scr[...] = jnp.zeros(acc_scr.shape, f32)

    acc_scr[...] += jnp.dot(x_ref[...].astype(bf16), w_ref[...].astype(bf16), preferred_element_type=f32)

    @pl.when(k == nk - 1)
    def _():
        o_ref[...] = _layer_norm(DN_ALPHA * r_ref[...] + acc_scr[...], g_ref[...], b_ref[...])


def _mm_post_norm(x, w, layer, res, g, b):
    M, K = x.shape
    D = w.shape[-1]
    tm = _row_tile(M)
    tk = min(MMLN_TK, K)
    nk = K // tk
    row = pl.BlockSpec((tm, D), lambda i, k: (i, 0))
    vec = pl.BlockSpec((1, D), lambda i, k: (0, 0))
    return pl.pallas_call(
        functools.partial(_mm_ln_kernel, nk=nk),
        grid=(pl.cdiv(M, tm), nk),
        in_specs=[
            pl.BlockSpec((tm, tk), lambda i, k: (i, k)),
            pl.BlockSpec((None, tk, D), lambda i, k: (layer, k, 0)),
            row, vec, vec,
        ],
        out_specs=row,
        out_shape=jax.ShapeDtypeStruct((M, D), f32),
        scratch_shapes=[pltpu.VMEM((tm, D), f32)],
        compiler_params=_cparams(("parallel", "arbitrary")),
        name="mm_post_norm",
    )(x, w, res, g.reshape(1, D), b.reshape(1, D))


def _post_norm_combine(x, ya, yb, w, g, b):
    M, D = x.shape
    tm = _row_tile(M, 128)
    row = pl.BlockSpec((tm, D), lambda i: (i, 0))
    vec = pl.BlockSpec((1, D), lambda i: (0, 0))
    return pl.pallas_call(
        _pn_combine_kernel,
        grid=(pl.cdiv(M, tm),),
        in_specs=[row, row, row, pl.BlockSpec((tm, TOP_K), lambda i: (i, 0)), vec, vec],
        out_specs=[row, row],
        out_shape=[jax.ShapeDtypeStruct((M, D), f32), jax.ShapeDtypeStruct((M, D), bf16)],
        compiler_params=_cparams(("parallel",)),
        name="post_norm_combine",
    )(x, ya, yb, w, g.reshape(1, D), b.reshape(1, D))


RET_HB = 2


def _ret_kernel(q_ref, k_ref, v_ref, g_ref, cos_ref, sin_ref, intra_ref, qd_ref, kd_ref, cd_ref, s0_ref,
                o_ref, sfin_ref, s_scr, *, nc):
    c = pl.program_id(2)

    @pl.when(c == 0)
    def _():
        s_scr[...] = s0_ref[...]

    half = RET_DK // 2
    cos = cos_ref[...]
    sin = sin_ref[...]

    def rot(t):
        t1, t2 = t[:, :half], t[:, half:]
        return jnp.concatenate([t1 * cos - t2 * sin, t2 * cos + t1 * sin], axis=-1)

    for hh in range(RET_HB):
        kcol = slice(hh * RET_DK, (hh + 1) * RET_DK)
        vcol = slice(hh * RET_DV, (hh + 1) * RET_DV)
        q = rot(q_ref[:, kcol])
        k = rot(k_ref[:, kcol]) * (RET_DK ** -0.5)
        v = v_ref[:, vcol].astype(bf16)
        s = s_scr[hh]
        att = _bdot_nt(q, k) * intra_ref[hh]
        o = _bdot(att, v) + _bdot(q * qd_ref[hh], s)
        kd = (k * kd_ref[hh]).T
        s_scr[hh] = s * cd_ref[hh] + _bdot(kd, v)
        mu = jnp.mean(o, -1, keepdims=True)
        d = o - mu
        var = jnp.mean(d * d, -1, keepdims=True)
        gate = g_ref[:, vcol]
        o_ref[:, vcol] = (gate * jax.nn.sigmoid(gate) * (d * lax.rsqrt(var + LN_EPS))).astype(o_ref.dtype)

    @pl.when(c == nc - 1)
    def _():
        sfin_ref[...] = s_scr[...]


def _ret_decay(chunk, valid):
    log_g = jnp.log(1.0 - 2.0 ** (-5.0 - jnp.arange(RET_HEADS, dtype=f32)))
    i = jnp.arange(chunk, dtype=f32)
    real = i < valid
    diff = i[:, None] - i[None, :]
    intra = jnp.where((diff >= 0) & real[None, :], jnp.exp(log_g[:, None, None] * jnp.maximum(diff, 0.0)), 0.0)
    q_dec = jnp.exp(log_g[:, None] * (i + 1.0))[:, :, None]
    k_dec = jnp.where(real, jnp.exp(log_g[:, None] * jnp.maximum(valid - 1.0 - i, 0.0)), 0.0)[:, :, None]
    c_dec = jnp.exp(log_g * valid)[:, None, None]
    return intra, q_dec, k_dec, c_dec


def _ret_tables(pos):
    inv = 1.0 / (RET_THETA ** jnp.linspace(0.0, 1.0, RET_DK // 2, dtype=f32))
    ang = pos.astype(f32)[:, None] * inv[None, :]
    return jnp.cos(ang), jnp.sin(ang)


def _retention(proj, pos, s0, nb, nc, chunk, valid, row0, o_prev):
    Ttot = proj.shape[0]
    hk, hv = RET_HEADS * RET_DK, RET_HEADS * RET_DV
    rb0 = row0 // chunk
    cos, sin = _ret_tables(pos)
    intra, q_dec, k_dec, c_dec = _ret_decay(chunk, valid)
    H = RET_HEADS

    def rowblk(b, h, c):
        return rb0 + b * nc + c

    HB = RET_HB
    wk, wv = HB * RET_DK, HB * RET_DV
    in_specs = [
        pl.BlockSpec((chunk, wk), lambda b, h, c: (rowblk(b, h, c), h)),
        pl.BlockSpec((chunk, wk), lambda b, h, c: (rowblk(b, h, c), hk // wk + h)),
        pl.BlockSpec((chunk, wv), lambda b, h, c: (rowblk(b, h, c), (2 * hk) // wv + h)),
        pl.BlockSpec((chunk, wv), lambda b, h, c: (rowblk(b, h, c), (2 * hk + hv) // wv + h)),
        pl.BlockSpec((chunk, RET_DK // 2), lambda b, h, c: (c, 0)),
        pl.BlockSpec((chunk, RET_DK // 2), lambda b, h, c: (c, 0)),
        pl.BlockSpec((HB, chunk, chunk), lambda b, h, c: (h, 0, 0)),
        pl.BlockSpec((HB, chunk, 1), lambda b, h, c: (h, 0, 0)),
        pl.BlockSpec((HB, chunk, 1), lambda b, h, c: (h, 0, 0)),
        pl.BlockSpec((HB, 1, 1), lambda b, h, c: (h, 0, 0)),
        pl.BlockSpec((None, HB, RET_DK, RET_DV), lambda b, h, c: (b, h, 0, 0)),
    ]
    args = [proj, proj, proj, proj, cos, sin, intra, q_dec, k_dec, c_dec, s0]
    aliases = {}
    if o_prev is not None:
        in_specs.append(pl.BlockSpec(memory_space=pl.ANY))
        args.append(o_prev)
        aliases = {len(args) - 1: 0}

    def body(*refs):
        if o_prev is not None:
            refs = refs[:11] + refs[12:]
        _ret_kernel(*refs, nc=nc)

    return pl.pallas_call(
        body,
        grid=(nb, H // HB, nc),
        in_specs=in_specs,
        out_specs=[
            pl.BlockSpec((chunk, wv), lambda b, h, c: (rowblk(b, h, c), h)),
            pl.BlockSpec((None, HB, RET_DK, RET_DV), lambda b, h, c: (b, h, 0, 0)),
        ],
        out_shape=[jax.ShapeDtypeStruct((Ttot, hv), bf16), jax.ShapeDtypeStruct((nb, H, RET_DK, RET_DV), f32)],
        scratch_shapes=[pltpu.VMEM((HB, RET_DK, RET_DV), f32)],
        input_output_aliases=aliases,
        compiler_params=_cparams(("parallel", "parallel", "arbitrary")),
        name="retention",
    )(*args)


def _moe_kernel(be_ref, nu_ref, src_ref, cnt_ref, x_hbm, *refs, nf):
    nt = MOE_NT
    wg_refs, wu_refs, wd_refs = refs[:nt], refs[nt:2 * nt], refs[2 * nt:3 * nt]
    o_ref, xg_scr, xb_scr, sem = refs[3 * nt:]
    i = pl.program_id(0)
    j = pl.program_id(1)
    n_used = nu_ref[0]

    def gather_rows(blk, slot):
        def start(r, carry):
            row = src_ref[blk * MOE_TM + r]
            pltpu.make_async_copy(x_hbm.at[pl.ds(row, 1)], xg_scr.at[slot, pl.ds(r, 1)], sem.at[slot]).start()
            return carry
        lax.fori_loop(0, MOE_TM, start, 0, unroll=8)

    @pl.when((i < n_used) & (j == 0))
    def _():
        slot = i % 2

        @pl.when(i == 0)
        def _():
            gather_rows(0, 0)

        pltpu.make_async_copy(x_hbm.at[pl.ds(0, MOE_TM)], xg_scr.at[slot], sem.at[slot]).wait()

        @pl.when(i + 1 < n_used)
        def _():
            gather_rows(i + 1, 1 - slot)

        xb_scr[...] = xg_scr[slot].astype(bf16)
        o_ref[...] = jnp.zeros(o_ref.shape, f32)

    def expert_mlp(m):
        xb = xb_scr[:m]
        wg = jnp.concatenate([r[...].astype(bf16) for r in wg_refs], axis=1)
        wu = jnp.concatenate([r[...].astype(bf16) for r in wu_refs], axis=1)
        wd = jnp.concatenate([r[...].astype(bf16) for r in wd_refs], axis=0)
        g = jnp.dot(xb, wg, preferred_element_type=f32)
        u = jnp.dot(xb, wu, preferred_element_type=f32)
        h = g * jax.nn.sigmoid(g) * u
        live = jnp.minimum(nf - nt * j, nt) * MOE_TF
        h = jnp.where(lax.broadcasted_iota(jnp.int32, h.shape, 1) < live, h, 0.0)
        o_ref[:m] += jnp.dot(h.astype(bf16), wd, preferred_element_type=f32)

    half_full = cnt_ref[jnp.minimum(i, n_used - 1)] <= MOE_TM // 2

    @pl.when((i < n_used) & jnp.logical_not(half_full))
    def _():
        expert_mlp(MOE_TM)

    @pl.when((i < n_used) & half_full)
    def _():
        expert_mlp(MOE_TM // 2)


def _moe_experts(x, rows_src, blk_e, blk_rows, n_used, w_g, w_u, w_d, layer):
    D = x.shape[1]
    R = rows_src.shape[0]
    n_blk = R // MOE_TM
    nf = D_EXPERT // MOE_TF
    nt = MOE_NT
    nsteps = -(-nf // nt)

    def ii(i, nu):
        return jnp.minimum(i, nu[0] - 1)

    def jj(i, j, nu, t):
        return jnp.minimum(nt * jnp.where(i < nu[0], j, nsteps - 1) + t, nf - 1)

    def w_in_spec(t):
        return pl.BlockSpec((None, None, D, MOE_TF),
                            lambda i, j, be, nu, src, cnt: (layer, be[ii(i, nu)], 0, jj(i, j, nu, t)))

    def w_out_spec(t):
        return pl.BlockSpec((None, None, MOE_TF, D),
                            lambda i, j, be, nu, src, cnt: (layer, be[ii(i, nu)], jj(i, j, nu, t), 0))

    grid_spec = pltpu.PrefetchScalarGridSpec(
        num_scalar_prefetch=4,
        grid=(n_blk, nsteps),
        in_specs=([pl.BlockSpec(memory_space=pl.ANY)] + [w_in_spec(t) for t in range(nt)] * 2
                  + [w_out_spec(t) for t in range(nt)]),
        out_specs=pl.BlockSpec((MOE_TM, D), lambda i, j, be, nu, src, cnt: (ii(i, nu), 0)),
        scratch_shapes=[
            pltpu.VMEM((2, MOE_TM, D), f32),
            pltpu.VMEM((MOE_TM, D), bf16),
            pltpu.SemaphoreType.DMA((2,)),
        ],
    )
    return pl.pallas_call(
        functools.partial(_moe_kernel, nf=nf),
        grid_spec=grid_spec,
        out_shape=jax.ShapeDtypeStruct((R, D), f32),
        compiler_params=_cparams(("arbitrary", "arbitrary")),
        name="moe_experts",
    )(blk_e, n_used, rows_src, blk_rows, x, *([w_g] * nt + [w_u] * nt + [w_d] * nt))


def _topk_small(x, k):
    lane = jnp.arange(x.shape[-1])
    vals, idxs = [], []
    for _ in range(k):
        i = jnp.argmax(x, axis=-1)
        vals.append(jnp.max(x, axis=-1))
        idxs.append(i)
        x = jnp.where(lane == i[..., None], -jnp.inf, x)
    return jnp.stack(vals, -1), jnp.stack(idxs, -1)


def _moe(x, w_router, b_router, w_g, w_u, w_d, layer):
    T, D = x.shape
    logits = _mm(x, w_router, body=_mm3_kernel)
    aff = jax.nn.sigmoid(logits)
    biased = aff + b_router.astype(f32)
    grp = biased.reshape(T, N_GROUPS, EXPERTS_PER_GROUP)
    grp_score = _topk_small(grp, 2)[0].sum(-1)
    _, top_g = _topk_small(grp_score, TOPK_GROUPS)
    g_mask = jnp.any(top_g[:, :, None] == jnp.arange(N_GROUPS)[None, None, :], axis=1)
    e_mask = jnp.repeat(g_mask, EXPERTS_PER_GROUP, axis=1)
    _, idx = _topk_small(jnp.where(e_mask, biased, -jnp.inf), TOP_K)
    w = jnp.take_along_axis(aff, idx, axis=1)
    w = w / jnp.sum(w, -1, keepdims=True)
    A = T * TOP_K
    e_flat = idx.reshape(A).astype(jnp.int32)
    order = jnp.argsort(e_flat)
    hot = (e_flat[:, None] == jnp.arange(N_EXPERTS, dtype=jnp.int32)[None, :]).astype(jnp.int32)
    counts = jnp.sum(hot, axis=0)
    start = jnp.cumsum(counts) - counts
    padded = (counts + MOE_TM - 1) // MOE_TM * MOE_TM
    pend = jnp.cumsum(padded)
    pstart = pend - padded
    n_blk = -(-A // MOE_TM) + N_EXPERTS
    blk_e = jnp.minimum(jnp.searchsorted(pend, jnp.arange(n_blk) * MOE_TM, side='right'), N_EXPERTS - 1)
    blk_e = blk_e.astype(jnp.int32)
    n_used = (pend[-1] // MOE_TM).astype(jnp.int32).reshape(1)
    blk_rows = jnp.clip(counts[blk_e] - (jnp.arange(n_blk) * MOE_TM - pstart[blk_e]), 0, MOE_TM).astype(jnp.int32)
    slot = jnp.arange(n_blk * MOE_TM, dtype=jnp.int32)
    e_slot = blk_e[slot // MOE_TM]
    rank = slot - pstart[e_slot]
    tok = order[jnp.clip(start[e_slot] + rank, 0, A - 1)] // TOP_K
    rows_src = jnp.where(rank < counts[e_slot], tok, slot % T).astype(jnp.int32)
    out = _moe_experts(x, rows_src, blk_e, blk_rows, n_used, w_g, w_u, w_d, layer)
    rank_a = jnp.sum(jnp.cumsum(hot, axis=0) * hot, axis=1) - 1
    pos = (pstart[e_flat] + rank_a).astype(jnp.int32).reshape(T, TOP_K)
    return out[pos[:, 0]], out[pos[:, 1]], w


def _nsa_inv_freq():
    return ROPE_THETA ** (-jnp.arange(0, ROT_DIM, 2, dtype=f32) / ROT_DIM)


def _rope_heads(x, pos):
    T = x.shape[0]
    half = ROT_DIM // 2
    ang = pos.astype(f32)[:, None] * _nsa_inv_freq()[None, :]
    cos = jnp.cos(ang)[:, None, :]
    sin = jnp.sin(ang)[:, None, :]
    xh = x.reshape(T, -1, NSA_HD)
    x1, x2 = xh[..., :half], xh[..., half:ROT_DIM]
    out = jnp.concatenate([x1 * cos - x2 * sin, x2 * cos + x1 * sin, xh[..., ROT_DIM:]], axis=-1)
    return out.reshape(x.shape)


def _masked_softmax(s, mask):
    s = jnp.where(mask, s, NEG)
    m = jnp.max(s, axis=-1, keepdims=True)
    m = jnp.where(m <= 0.5 * NEG, 0.0, m)
    e = jnp.where(mask, jnp.exp(s - m), 0.0)
    den = jnp.sum(e, axis=-1, keepdims=True)
    return e / jnp.where(den > 0, den, 1.0)


def _compress(chunks, w1, w2, pe):
    n, nchunk, kk = chunks.shape
    wab = jnp.concatenate([w1[:kk], w1[kk:]], axis=1)
    y = _mm(chunks.reshape(n * nchunk, kk), wab).reshape(n, nchunk, 2 * NSA_HD)
    bias = _mm(jnp.broadcast_to(pe.reshape(1, CMP_LEN * NSA_HD), (8, CMP_LEN * NSA_HD)), w1)[:1]
    pre = y[:, :-1, :NSA_HD] + y[:, 1:, NSA_HD:] + bias
    nc = nchunk - 1
    return _mm(jax.nn.silu(pre).reshape(n * nc, NSA_HD), w2).reshape(n, nc, NSA_HD)


CMP_P = 8


def _cmp_pages_kernel(pt_ref, *refs):
    P = CMP_P
    page_refs, w_ref, o_ref, x_scr = refs[:2 * P], refs[2 * P], refs[2 * P + 1], refs[2 * P + 2]
    cpp = PAGE_SIZE // CMP_STRIDE
    for typ in range(2):
        for r in range(CMP_STRIDE):
            for g in range(NSA_KV):
                for p in range(P):
                    row = (g * P + p) * cpp
                    x_scr[row:row + cpp, r * NSA_HD:(r + 1) * NSA_HD] = (
                        page_refs[typ * P + p][pl.ds(r, cpp, stride=CMP_STRIDE), g, :])
        y = _bdot(x_scr[...], w_ref[typ])
        for g in range(NSA_KV):
            o_ref[typ, g] = y[g * P * cpp:(g + 1) * P * cpp]


def _cmp_pages(cache_kv, page_table, wab):
    DB, n_pages = page_table.shape
    P = CMP_P
    G, d = NSA_KV, NSA_HD
    cpp = PAGE_SIZE // CMP_STRIDE
    kk = CMP_STRIDE * d

    def page_map(p, typ):
        return lambda b, j, pt: (pt[b * n_pages + j * P + p], 0, typ, 0, 0)

    in_specs = [pl.BlockSpec((None, PAGE_SIZE, None, G, d), page_map(p, typ)) for typ in range(2) for p in range(P)]
    in_specs.append(pl.BlockSpec((2, kk, 2 * d), lambda b, j, pt: (0, 0, 0)))
    return pl.pallas_call(
        _cmp_pages_kernel,
        grid_spec=pltpu.PrefetchScalarGridSpec(
            num_scalar_prefetch=1,
            grid=(DB, n_pages // P),
            in_specs=in_specs,
            out_specs=pl.BlockSpec((None, 2, G, P * cpp, 2 * d), lambda b, j, pt: (b, 0, 0, j, 0)),
            scratch_shapes=[pltpu.VMEM((G * P * cpp, kk), f32)],
        ),
        out_shape=jax.ShapeDtypeStruct((DB, 2, G, n_pages * cpp, 2 * d), f32),
        compiler_params=_cparams(("arbitrary", "arbitrary")),
        name="cmp_pages",
    )(page_table.reshape(-1).astype(jnp.int32), *([cache_kv] * (2 * P)), wab)


def _compress_cache(cache_kv, page_table, w_k1, w_k2, pe_k, w_v1, w_v2, pe_v):
    DB = page_table.shape[0]
    G, d = NSA_KV, NSA_HD
    kk = CMP_STRIDE * d
    wab = jnp.stack([jnp.concatenate([w[:kk], w[kk:]], axis=1) for w in (w_k1, w_v1)])
    y = _cmp_pages(cache_kv, page_table, wab)
    out = []
    for typ, (w1, w2, pe) in enumerate(((w_k1, w_k2, pe_k), (w_v1, w_v2, pe_v))):
        bias = _mm(jnp.broadcast_to(pe.reshape(1, CMP_LEN * d), (8, CMP_LEN * d)), w1)[:1]
        pre = y[:, typ, :, :-1, :d] + y[:, typ, :, 1:, d:] + bias
        nc = pre.shape[2]
        c = _mm(jax.nn.silu(pre).reshape(DB * G * nc, d), w2)
        out.append(c.reshape(DB, G, nc, d).transpose(0, 2, 1, 3))
    return out


def _cmp_kernel(q_ref, kc_ref, vc_ref, ov_ref, o_ref, sc_ref):
    qi = pl.program_id(2)
    ncp = kc_ref.shape[0]
    H, d = NSA_HPG, NSA_HD
    rows = lax.broadcasted_iota(jnp.int32, (H * QBLK, ncp), 0)
    qpos = qi * QBLK + rows % QBLK
    cend = lax.broadcasted_iota(jnp.int32, (H * QBLK, ncp), 1) * CMP_STRIDE + (CMP_LEN - 1)
    mask = cend <= qpos
    kc = kc_ref[...].astype(bf16)
    vc = vc_ref[...].astype(bf16)
    q4 = jnp.concatenate([q_ref[:, h * d:(h + 1) * d].astype(bf16) for h in range(H)], axis=0)
    s = _bdot_nt(q4, kc) * (d ** -0.5)
    p = _masked_softmax(s, mask).astype(bf16)
    o4 = jnp.dot(p, vc, preferred_element_type=f32)
    sc4 = jnp.dot(p, ov_ref[...], preferred_element_type=f32)
    for h in range(H):
        o_ref[:, h * d:(h + 1) * d] = o4[h * QBLK:(h + 1) * QBLK]
    sc_ref[...] = functools.reduce(jnp.add, [sc4[h * QBLK:(h + 1) * QBLK] for h in range(H)])


def _cmp_prompt(q, kc, vc, overlap, B, S):
    Ttot = q.shape[0]
    nq = S // QBLK
    ncp = kc.shape[1]
    ns = overlap.shape[1]
    G = NSA_KV
    gw = NSA_HPG * NSA_HD
    return pl.pallas_call(
        _cmp_kernel,
        grid=(B, G, nq),
        in_specs=[
            pl.BlockSpec((QBLK, gw), lambda b, g, i: (b * nq + i, g)),
            pl.BlockSpec((None, ncp, NSA_HD), lambda b, g, i: (b * G + g, 0, 0)),
            pl.BlockSpec((None, ncp, NSA_HD), lambda b, g, i: (b * G + g, 0, 0)),
            pl.BlockSpec((ncp, ns), lambda b, g, i: (0, 0)),
        ],
        out_specs=[
            pl.BlockSpec((QBLK, gw), lambda b, g, i: (b * nq + i, g)),
            pl.BlockSpec((None, QBLK, ns), lambda b, g, i: (b * G + g, i, 0)),
        ],
        out_shape=[jax.ShapeDtypeStruct((Ttot, G * gw), f32), jax.ShapeDtypeStruct((B * G, S, ns), f32)],
        compiler_params=_cparams(("parallel", "parallel", "parallel")),
        name="cmp_prompt",
    )(q, kc, vc, overlap)


SEL_TK = 512


LOG2E = 1.4426950408889634
LANES = 128


def _fold_lanes(x, op):
    parts = [x[:, j * LANES:(j + 1) * LANES] for j in range(x.shape[1] // LANES)]
    return functools.reduce(op, parts)


def _slc_win_kernel(q_ref, rc_ref, rs1_ref, rs2_ref, sel_ref, ks_ref, vs_ref, kw_ref, vw_ref, oc_ref, gt_ref, o_ref,
                    s_scr, m_scr, l_scr, acc_scr):
    qi = pl.program_id(2)
    H, d = NSA_HPG, NSA_HD
    c2 = (d ** -0.5) * LOG2E
    rc, rs1, rs2 = rc_ref[...], rs1_ref[...], rs2_ref[...]
    half = ROT_DIM // 2

    def roped(h):
        x = q_ref[:, h * d:(h + 1) * d]
        return x * rc + pltpu.roll(x, d - half, 1) * rs1 + pltpu.roll(x, half, 1) * rs2

    q4 = jnp.concatenate([(roped(h) * c2).astype(bf16) for h in range(H)], axis=0)
    sel = sel_ref[...].astype(bf16)
    ns = sel.shape[1]
    bpt = SEL_TK // SLC_LEN
    nl = SEL_TK // LANES

    m_scr[...] = jnp.full(m_scr.shape, NEG, f32)
    l_scr[...] = jnp.zeros(l_scr.shape, f32)
    acc_scr[...] = jnp.zeros(acc_scr.shape, f32)
    qpos = qi * QBLK + lax.broadcasted_iota(jnp.int32, (QBLK, SEL_TK), 0)
    n_tiles = (qi * QBLK + QBLK - 1) // SEL_TK + 1

    def scores(t, carry):
        k0 = pl.multiple_of(t * SEL_TK, SEL_TK)
        k = ks_ref[pl.ds(k0, SEL_TK), :].astype(bf16)
        blk = lax.broadcasted_iota(jnp.int32, (ns, SEL_TK), 1) // SLC_LEN + t * bpt
        expand = jnp.where(blk == lax.broadcasted_iota(jnp.int32, (ns, SEL_TK), 0), 1.0, 0.0).astype(bf16)
        chosen = jnp.dot(sel, expand, preferred_element_type=f32) > 0.5
        kpos = k0 + lax.broadcasted_iota(jnp.int32, (QBLK, SEL_TK), 1)
        bias = jnp.where(chosen & (kpos <= qpos), 0.0, NEG)
        s4 = _bdot_nt(q4, k)
        for h in range(H):
            rows = slice(h * QBLK, (h + 1) * QBLK)
            s = s4[rows] + bias
            s_scr[t, rows] = s
            m_scr[rows] = jnp.maximum(m_scr[rows], _fold_lanes(s, jnp.maximum))
        return carry

    lax.fori_loop(0, n_tiles, scores, 0)
    m_scr[...] = jnp.broadcast_to(jnp.max(m_scr[...], axis=-1, keepdims=True), m_scr.shape)

    def accumulate(t, carry):
        k0 = pl.multiple_of(t * SEL_TK, SEL_TK)
        v = vs_ref[pl.ds(k0, SEL_TK), :].astype(bf16)
        s = s_scr[t]
        m = m_scr[...]
        p = jnp.concatenate([jnp.exp2(s[:, j * LANES:(j + 1) * LANES] - m) for j in range(nl)], axis=-1)
        l_scr[...] += _fold_lanes(p, jnp.add)
        acc_scr[...] += jnp.dot(p.astype(bf16), v, preferred_element_type=f32)
        return carry

    lax.fori_loop(0, n_tiles, accumulate, 0)
    o_slc = acc_scr[...] / jnp.sum(l_scr[...], axis=-1, keepdims=True)

    wk = WINDOW + QBLK
    w0 = pl.multiple_of(jnp.maximum(qi * QBLK - WINDOW, 0), QBLK)
    kw = kw_ref[pl.ds(w0, wk), :].astype(bf16)
    vw = vw_ref[pl.ds(w0, wk), :].astype(bf16)
    qp = qi * QBLK + lax.broadcasted_iota(jnp.int32, (QBLK, wk), 0)
    kp = w0 + lax.broadcasted_iota(jnp.int32, (QBLK, wk), 1)
    wbias = jnp.where((kp <= qp) & (kp >= qp - WINDOW), 0.0, NEG)
    sw = _bdot_nt(q4, kw)
    sw = jnp.concatenate([sw[h * QBLK:(h + 1) * QBLK] + wbias for h in range(H)], axis=0)
    pw = jnp.exp2(sw - jnp.max(sw, axis=-1, keepdims=True))
    o_win = jnp.dot(pw.astype(bf16), vw, preferred_element_type=f32) / jnp.sum(pw, axis=-1, keepdims=True)

    gt = gt_ref[...]
    for h in range(H):
        rows = slice(h * QBLK, (h + 1) * QBLK)
        o = (gt[:, 3 * h:3 * h + 1] * oc_ref[:, h * d:(h + 1) * d]
             + gt[:, 3 * h + 1:3 * h + 2] * o_slc[rows] + gt[:, 3 * h + 2:3 * h + 3] * o_win[rows])
        o_ref[:, h * d:(h + 1) * d] = o.astype(o_ref.dtype)


def _rope_tables(pos):
    half = ROT_DIM // 2
    ang = pos.astype(f32)[:, None] * _nsa_inv_freq()[None, :]
    cos, sin = jnp.cos(ang), jnp.sin(ang)
    n = pos.shape[0]
    c = jnp.concatenate([cos, cos, jnp.ones((n, NSA_HD - ROT_DIM), f32)], axis=1)
    s1 = jnp.concatenate([-sin, jnp.zeros((n, NSA_HD - half), f32)], axis=1)
    s2 = jnp.concatenate([jnp.zeros((n, half), f32), sin, jnp.zeros((n, NSA_HD - ROT_DIM), f32)], axis=1)
    return c, s1, s2


KV_ROPED = (2, 4)


def _mm_kv_kernel(x_ref, w_ref, rc_ref, rs1_ref, rs2_ref, o_ref, wb_scr):
    j = pl.program_id(0)

    @pl.when(pl.program_id(1) == 0)
    def _():
        wb_scr[...] = w_ref[...].astype(bf16)

    y = jnp.dot(x_ref[...], wb_scr[...], preferred_element_type=f32)
    roped = functools.reduce(jnp.logical_or, [j == t for t in KV_ROPED])

    @pl.when(jnp.logical_not(roped))
    def _():
        o_ref[...] = y

    @pl.when(roped)
    def _():
        rc, rs1, rs2 = rc_ref[...], rs1_ref[...], rs2_ref[...]
        half = ROT_DIM // 2
        for g in range(NSA_KV):
            x = y[:, g * NSA_HD:(g + 1) * NSA_HD]
            o_ref[:, g * NSA_HD:(g + 1) * NSA_HD] = (
                x * rc + pltpu.roll(x, NSA_HD - half, 1) * rs1 + pltpu.roll(x, half, 1) * rs2)


def _mm_kv(xb, w_kv, pos):
    M, K = xb.shape
    N = w_kv.shape[1]
    tn = NSA_KV * NSA_HD
    tm = _row_tile(M)
    rc, rs1, rs2 = _rope_tables(pos)
    tspec = pl.BlockSpec((tm, NSA_HD), lambda j, i: (i, 0))
    return pl.pallas_call(
        _mm_kv_kernel,
        grid=(N // tn, pl.cdiv(M, tm)),
        in_specs=[pl.BlockSpec((tm, K), lambda j, i: (i, 0)), pl.BlockSpec((K, tn), lambda j, i: (0, j)),
                  tspec, tspec, tspec],
        out_specs=pl.BlockSpec((tm, tn), lambda j, i: (i, j)),
        out_shape=jax.ShapeDtypeStruct((M, N), f32),
        scratch_shapes=[pltpu.VMEM((K, tn), bf16)],
        compiler_params=_cparams(("parallel", "arbitrary")),
        name="mm_kv",
    )(xb, w_kv, rc, rs1, rs2)


def _slc_win_prompt(q, pos, selmask, kv, o_cmp, gates, B, S):
    Ttot = q.shape[0]
    nq = S // QBLK
    ns = selmask.shape[-1]
    G = NSA_KV
    gw = NSA_HPG * NSA_HD
    rc, rs1, rs2 = _rope_tables(pos)
    qspec = pl.BlockSpec((QBLK, gw), lambda b, g, i: (b * nq + i, g))
    tspec = pl.BlockSpec((QBLK, NSA_HD), lambda b, g, i: (i, 0))
    return pl.pallas_call(
        _slc_win_kernel,
        grid=(B, G, nq),
        in_specs=[
            qspec, tspec, tspec, tspec,
            pl.BlockSpec((None, QBLK, ns), lambda b, g, i: (b * G + g, i, 0)),
            pl.BlockSpec((S, NSA_HD), lambda b, g, i: (b, 2 * G + g)),
            pl.BlockSpec((S, NSA_HD), lambda b, g, i: (b, 3 * G + g)),
            pl.BlockSpec((S, NSA_HD), lambda b, g, i: (b, 4 * G + g)),
            pl.BlockSpec((S, NSA_HD), lambda b, g, i: (b, 5 * G + g)),
            qspec,
            pl.BlockSpec((None, QBLK, 3 * NSA_HPG), lambda b, g, i: (g, b * nq + i, 0)),
        ],
        out_specs=qspec,
        out_shape=jax.ShapeDtypeStruct((Ttot, G * gw), bf16),
        scratch_shapes=[
            pltpu.VMEM((S // SEL_TK, NSA_HPG * QBLK, SEL_TK), f32),
            pltpu.VMEM((NSA_HPG * QBLK, LANES), f32),
            pltpu.VMEM((NSA_HPG * QBLK, LANES), f32),
            pltpu.VMEM((NSA_HPG * QBLK, NSA_HD), f32),
        ],
        compiler_params=_cparams(("parallel", "parallel", "arbitrary")),
        name="slc_win_prompt",
    )(q, rc, rs1, rs2, selmask, kv, kv, kv, kv, o_cmp, gates)


def _overlap(nc, ns):
    c0 = jnp.arange(nc) * CMP_STRIDE
    j0 = jnp.arange(ns) * SLC_LEN
    return ((c0[:, None] < j0[None, :] + SLC_LEN) & (c0[:, None] + CMP_LEN > j0[None, :]))


def _select(score, qpos, ns):
    tblk = qpos // SLC_LEN
    j = jnp.arange(ns)
    forced = (j[None, :] == 0) | (j[None, :] == tblk[:, None]) | (j[None, :] == tblk[:, None] - 1)
    score = jnp.where(forced, jnp.inf, jnp.where(j[None, :] > tblk[:, None], -jnp.inf, score))
    _, idx = lax.top_k(score, min(N_SEL, ns))
    return idx


def _cmp_attention_s(q, kc, vc, qpos):
    nc = kc.shape[1]
    cend = jnp.arange(nc) * CMP_STRIDE + (CMP_LEN - 1)
    mask = cend[None, :] <= qpos[:, None]
    s = jnp.einsum('bqghd,bcgd->bghqc', q, kc, preferred_element_type=f32) * (NSA_HD ** -0.5)
    p = _masked_softmax(s, mask)
    o = jnp.einsum('bghqc,bcgd->bqghd', p, vc)
    return o, p


def _attend_s(q, k, v, mask):
    s = jnp.einsum('bqghd,bkgd->bghqk', q, k, preferred_element_type=f32) * (NSA_HD ** -0.5)
    p = _masked_softmax(s, mask[None, None, None, :, :])
    return jnp.einsum('bghqk,bkgd->bqghd', p, v)


HPAD = 16


def _slc_s_kernel(sel_ref, pt_ref, tsel_ref, q_ref, tk_ref, tv_ref, *refs, n_sel, npb, nt, past, T):
    k_refs, v_refs, o_ref = refs[:n_sel], refs[n_sel:2 * n_sel], refs[2 * n_sel]
    k_scr, v_scr = refs[2 * n_sel + 1], refs[2 * n_sel + 2]
    b, g, t = pl.program_id(0), pl.program_id(1), pl.program_id(2)
    flat = (b * NSA_KV + g) * T + t
    qpos = past + t
    L = SLC_LEN
    nkeys = k_scr.shape[0]
    part = lax.broadcasted_iota(jnp.int32, (HPAD, nkeys), 1) // L
    lim = jnp.full((HPAD, nkeys), -1, jnp.int32)
    for n in range(n_sel):
        bid = sel_ref[flat * n_sel + n]
        k_scr[n * L:(n + 1) * L, :] = k_refs[n][:, g, :]
        v_scr[n * L:(n + 1) * L, :] = v_refs[n][:, g, :]
        lim = jnp.where(part == n, jnp.where(bid < npb, qpos - bid * L, -1), lim)
    for j in range(nt):
        n = n_sel + j
        k_scr[n * L:(n + 1) * L, :] = tk_ref[j * L:(j + 1) * L, g, :]
        v_scr[n * L:(n + 1) * L, :] = tv_ref[j * L:(j + 1) * L, g, :]
        lim = jnp.where(part == n, jnp.where(tsel_ref[flat * nt + j] > 0, qpos - (npb + j) * L, -1), lim)
    used = (n_sel + nt) * L
    if used < nkeys:
        k_scr[used:, :] = jnp.zeros((nkeys - used, NSA_HD), f32)
        v_scr[used:, :] = jnp.zeros((nkeys - used, NSA_HD), f32)
    off = lax.broadcasted_iota(jnp.int32, (HPAD, nkeys), 1) % L
    s = _bdot_nt(q_ref[...], k_scr[...]) * (NSA_HD ** -0.5)
    p = _masked_softmax(s, off <= lim)
    o_ref[...] = _bdot(p, v_scr[...])


def _slc_sample(qr, sel, tail, cache_kv, page_table, past):
    DB, T, G, H, d = qr.shape
    n_sel = sel.shape[-1]
    sub = PAGE_SIZE // SLC_LEN
    npb = past // SLC_LEN
    nt = tail.shape[1] // SLC_LEN
    n_pages = page_table.shape[1]
    nkeys = -(-((n_sel + nt) * SLC_LEN) // LANES) * LANES
    q8 = jnp.pad(qr.transpose(0, 2, 1, 3, 4), ((0, 0), (0, 0), (0, 0), (0, HPAD - H), (0, 0)))
    q8 = q8.reshape(DB * G * T, HPAD, d)
    tsel = jnp.any(sel[..., None] == npb + jnp.arange(nt), axis=-2).astype(jnp.int32)

    def cache_map(n, typ):
        def f(b, g, t, sel_r, pt_r, ts_r):
            bid = sel_r[((b * G + g) * T + t) * n_sel + n]
            jp = jnp.clip(bid, 0, npb - 1)
            return (pt_r[b * n_pages + jp // sub], jp % sub, typ, 0, 0)
        return f

    in_specs = [
        pl.BlockSpec((None, HPAD, d), lambda b, g, t, *_: ((b * G + g) * T + t, 0, 0)),
        pl.BlockSpec((None, nt * SLC_LEN, None, G, d), lambda b, g, t, *_: (b, 0, 2, 0, 0)),
        pl.BlockSpec((None, nt * SLC_LEN, None, G, d), lambda b, g, t, *_: (b, 0, 3, 0, 0)),
    ]
    in_specs += [pl.BlockSpec((None, SLC_LEN, None, G, d), cache_map(n, 2)) for n in range(n_sel)]
    in_specs += [pl.BlockSpec((None, SLC_LEN, None, G, d), cache_map(n, 3)) for n in range(n_sel)]
    out = pl.pallas_call(
        functools.partial(_slc_s_kernel, n_sel=n_sel, npb=npb, nt=nt, past=past, T=T),
        grid_spec=pltpu.PrefetchScalarGridSpec(
            num_scalar_prefetch=3,
            grid=(DB, G, T),
            in_specs=in_specs,
            out_specs=pl.BlockSpec((None, HPAD, d), lambda b, g, t, *_: ((b * G + g) * T + t, 0, 0)),
            scratch_shapes=[pltpu.VMEM((nkeys, d), f32), pltpu.VMEM((nkeys, d), f32)],
        ),
        out_shape=jax.ShapeDtypeStruct((DB * G * T, HPAD, d), f32),
        compiler_params=_cparams(("arbitrary", "arbitrary", "arbitrary")),
        name="slc_sample",
    )(sel.reshape(-1), page_table.reshape(-1).astype(jnp.int32), tsel.reshape(-1), q8, tail, tail,
      *([cache_kv] * (2 * n_sel)))
    return out.reshape(DB, G, T, HPAD, d)[:, :, :, :H].transpose(0, 2, 1, 3, 4)


def _nsa_sample(q, qr, pos, rows, win_keys, kc, vc, cache_kv, page_table):
    DB, T = q.shape[:2]
    past = page_table.shape[1] * PAGE_SIZE
    sub = PAGE_SIZE // SLC_LEN
    npb = past // SLC_LEN
    ns = -(-(past + T) // SLC_LEN)
    nt = ns - npb
    tail = jnp.pad(rows, ((0, 0), (0, nt * SLC_LEN - T), (0, 0), (0, 0), (0, 0)))
    o_cmp, p_cmp = _cmp_attention_s(q, kc, vc, pos)
    nc = p_cmp.shape[-1]
    score = jnp.einsum('bghqc,cj->bgqj', p_cmp, _overlap(nc, ns).astype(f32))
    sel = _select(score, pos, ns).astype(jnp.int32)
    o_slc = _slc_sample(qr, sel, tail, cache_kv, page_table, past)
    wb = win_keys.shape[1] - T
    kpos_w = past - wb + jnp.arange(wb + T)
    wmask = (kpos_w[None, :] <= pos[:, None]) & (kpos_w[None, :] >= pos[:, None] - WINDOW)
    o_win = _attend_s(qr, win_keys[:, :, 0], win_keys[:, :, 1], wmask)
    return o_cmp, o_slc, o_win


def kernel(x_prompt, x_sample, cache_kv, cache_win, state_ret, page_table,
           w_ret_in, w_ret_out, w_kv, w_cmp_k1, w_cmp_k2, pe_cmp_k, w_cmp_v1, w_cmp_v2, pe_cmp_v,
           w_nsa_in, w_nsa_out, w_router, b_router, w_exp_gate, w_exp_up, w_exp_down, ln_gain, ln_bias):
    B, S, D = x_prompt.shape
    DB, T, _ = x_sample.shape
    past = page_table.shape[1] * PAGE_SIZE
    wb = cache_win.shape[1]
    G, HPG, hd = NSA_KV, NSA_HPG, NSA_HD
    P = B * S
    Ttot = P + DB * TP
    pos_p = jnp.arange(S, dtype=jnp.int32)
    pos_sp = past + jnp.arange(TP, dtype=jnp.int32)
    pos_s = pos_sp[:T]
    pos_flat = jnp.concatenate([jnp.tile(pos_p, B), jnp.tile(pos_sp, DB)])
    chunk = min(RET_CHUNK, S)

    xs_pad = jnp.pad(x_sample, ((0, 0), (0, TP - T), (0, 0)))
    xt = jnp.concatenate([x_prompt.reshape(P, D), xs_pad.reshape(DB * TP, D)], axis=0)
    xt_b = xt.astype(bf16)

    ret_p, ret_s = [], []
    for l in range(DEPTH):
        if l < N_A_LAYERS:
            proj = _mm(xt_b, w_ret_in, layer=l)
            s0 = jnp.zeros((B, RET_HEADS, RET_DK, RET_DV), f32)
            o, sp = _retention(proj, pos_p[:chunk * (S // chunk)], s0, B, S // chunk, chunk, float(chunk), 0, None)
            o, ss = _retention(proj, pos_sp, state_ret[l].astype(f32), DB, 1, TP, float(T), P, o)
            ret_p.append(sp.astype(state_ret.dtype))
            ret_s.append(ss.astype(state_ret.dtype))
            xt = _mm_post_norm(o, w_ret_out, l, xt, ln_gain[l, 0], ln_bias[l, 0])
        else:
            if l == N_A_LAYERS:
                kv = _mm_kv(xt_b, w_kv, pos_flat)
                gd = G * hd
                rows_p = kv[:P, :4 * gd].reshape(B, S, 4, G, hd)
                rows_s = kv[P:, :4 * gd].reshape(DB, TP, 4, G, hd)[:, :T]
                win_rows_s = kv[P:, 4 * gd:].reshape(DB, TP, 2, G, hd)[:, :T]
                win_keys_s = jnp.concatenate([cache_win, win_rows_s.astype(cache_win.dtype)], axis=1)

                def chunks_of(r):
                    n, N = r.shape[:2]
                    nch = N // CMP_STRIDE
                    c = r[:, :nch * CMP_STRIDE].reshape(n, nch, CMP_STRIDE, G, hd)
                    return c.transpose(0, 3, 1, 2, 4).reshape(n * G, nch, CMP_STRIDE * hd)

                def comp(r, w1, w2, pe):
                    n = r.shape[0]
                    c = _compress(chunks_of(r), w1, w2, pe)
                    return c.reshape(n, G, c.shape[1], hd).transpose(0, 2, 1, 3)

                kc_p = comp(rows_p[:, :, 0], w_cmp_k1, w_cmp_k2, pe_cmp_k)
                vc_p = comp(rows_p[:, :, 1], w_cmp_v1, w_cmp_v2, pe_cmp_v)
                assert (past + T) // CMP_STRIDE == past // CMP_STRIDE and page_table.shape[1] % CMP_P == 0
                kc_s, vc_s = _compress_cache(cache_kv, page_table, w_cmp_k1, w_cmp_k2, pe_cmp_k,
                                             w_cmp_v1, w_cmp_v2, pe_cmp_v)
                nc_p = kc_p.shape[1]
                ncp = -(-nc_p // 128) * 128
                ns_p = -(-S // SLC_LEN)

                def pad_c(c):
                    c = jnp.pad(c, ((0, 0), (0, ncp - nc_p), (0, 0), (0, 0)))
                    return c.transpose(0, 2, 1, 3).reshape(B * G, ncp, hd)

                kc_pp, vc_pp = pad_c(kc_p), pad_c(vc_p)
                ov_p = _overlap(ncp, ns_p).astype(bf16)
            li = l - N_A_LAYERS
            hq = NSA_HEADS * hd
            w_in = w_nsa_in[li]
            q = _mm(xt_b, w_in[:, :hq])
            gates = jax.nn.sigmoid(_mm(xt_b, w_in[:, hq:]))
            o_cmp, score = _cmp_prompt(q, kc_pp, vc_pp, ov_p, B, S)
            sel = _select(score.reshape(B, G, S, ns_p), pos_p, ns_p)
            selmask = jnp.sum(jax.nn.one_hot(sel, ns_p, dtype=f32), axis=-2).reshape(B * G, S, ns_p)
            gates_g = gates.reshape(Ttot, G, 3 * HPG).transpose(1, 0, 2)
            merged = _slc_win_prompt(q, pos_p, selmask, kv, o_cmp, gates_g, B, S)
            q_s = q[P:].reshape(DB, TP, G, HPG, hd)[:, :T]
            qr_s = _rope_heads(q[P:], pos_flat[P:]).reshape(DB, TP, G, HPG, hd)[:, :T]
            oc_s, os_s, ow_s = _nsa_sample(q_s, qr_s, pos_s, rows_s, win_keys_s, kc_s, vc_s, cache_kv, page_table)
            g_s = gates[P:].reshape(DB, TP, G, HPG, 3)[:, :T]
            m_s = g_s[..., 0:1] * oc_s + g_s[..., 1:2] * os_s + g_s[..., 2:3] * ow_s
            m_s = jnp.pad(m_s.reshape(DB, T, hq), ((0, 0), (0, TP - T), (0, 0))).reshape(DB * TP, hq)
            merged = lax.dynamic_update_slice(merged, m_s.astype(merged.dtype), (P, 0))
            xt = _mm_post_norm(merged, w_nsa_out, li, xt, ln_gain[l, 0], ln_bias[l, 0])
        ya, yb, wk = _moe(xt, w_router, b_router, w_exp_gate, w_exp_up, w_exp_down, l)
        xt, xt_b = _post_norm_combine(xt, ya, yb, wk, ln_gain[l, 1], ln_bias[l, 1])

    xp = xt[:P].reshape(B, S, D)
    xs = xt[P:].reshape(DB, TP, D)[:, :T]
    win_rows_p = kv[:P, 4 * G * hd:].reshape(B, S, 2, G, hd)
    if S >= wb:
        win_p = win_rows_p[:, S - wb:]
    else:
        win_p = jnp.pad(win_rows_p, ((0, 0), (wb - S, 0), (0, 0), (0, 0), (0, 0)))
    win_s = win_keys_s[:, T:]
    return (xp, xs, rows_p, rows_s, win_p, win_s, jnp.stack(ret_p), jnp.stack(ret_s))
```
